```python
import jax, jax.numpy as jnp
from jax import lax
import numpy as np

D_MODEL = 1024
BATCH = 4
SEQ = 8192
DEPTH = 4

BRANCH_WIDTH = D_MODEL // 2
N_BRANCHES = 4
ATT_HEADS = 8
ATT_KV_HEADS = 2
ATT_HEAD_DIM = BRANCH_WIDTH // ATT_HEADS
ROT_DIM = ATT_HEAD_DIM // 4
ROPE_THETA = 500000.0
IDX_HEADS = 4
IDX_HEAD_DIM = 64
INDEX_TOPK = 256
Q_BLOCK = 128
RWKV_HEAD = 64
RWKV_HEADS = BRANCH_WIDTH // RWKV_HEAD
RWKV_DECAY_LORA = 32
RWKV_A_LORA = 32
RWKV_GATE_LORA = 96
RWKV_GN_EPS = 64e-5
POOL_WINDOWS = (2, 4, 8, 16)
POOL_GROUPS = len(POOL_WINDOWS)
POOL_GROUP_DIM = BRANCH_WIDTH // POOL_GROUPS
RET_HEADS = 8
RET_KEY_DIM = BRANCH_WIDTH // 2 // RET_HEADS
RET_VALUE_DIM = BRANCH_WIDTH // RET_HEADS
RET_CHUNK = 128
RET_THETA = 10000.0
RET_GN_EPS = 1e-6
D_FF = 4 * D_MODEL
NORM_EPS = 1e-5

A_COLS = (ATT_HEADS * ATT_HEAD_DIM, ATT_KV_HEADS * ATT_HEAD_DIM, ATT_KV_HEADS * ATT_HEAD_DIM,
          IDX_HEADS * IDX_HEAD_DIM, IDX_HEAD_DIM, IDX_HEADS)
B_COLS = (BRANCH_WIDTH, RWKV_DECAY_LORA, BRANCH_WIDTH, BRANCH_WIDTH, RWKV_A_LORA, RWKV_GATE_LORA)
C_COLS = (BRANCH_WIDTH,)
D_COLS = (RET_HEADS * RET_KEY_DIM, RET_HEADS * RET_KEY_DIM, RET_HEADS * RET_VALUE_DIM, BRANCH_WIDTH)
N_A = sum(A_COLS)
N_B = sum(B_COLS)
N_C = sum(C_COLS)
N_D = sum(D_COLS)
N_G = N_BRANCHES * D_MODEL
N_IN = N_A + N_B + N_C + N_D + N_G

kernel_name = 'hybrid_gated_dsa_rwkv7_pool_retention'


def rms_norm(x, g):
    xf = x.astype(jnp.float32)
    y = xf * lax.rsqrt(jnp.mean(xf * xf, -1, keepdims=True) + NORM_EPS)
    return (y * g.astype(jnp.float32)).astype(x.dtype)


def split_cols(p, sizes):
    outs, off = [], 0
    for sz in sizes:
        outs.append(p[..., off:off + sz])
        off += sz
    return outs


def cos_sin(positions, inv_freq, dtype):
    ang = positions.astype(jnp.float32)[..., None] * inv_freq
    return jnp.cos(ang).astype(dtype)[:, :, None, :], jnp.sin(ang).astype(dtype)[:, :, None, :]


def rotate(x, cos, sin):
    half = x.shape[-1] // 2
    x1, x2 = x[..., :half], x[..., half:]
    return jnp.concatenate([x1 * cos - x2 * sin, x2 * cos + x1 * sin], -1)


def partial_rope(x, cos, sin):
    return jnp.concatenate([rotate(x[..., :ROT_DIM], cos, sin), x[..., ROT_DIM:]], -1)


def dsa_attention(q, k, v, q_idx, k_idx, w_idx):
    b, s = q.shape[0], q.shape[1]
    top_k = min(INDEX_TOPK, s // 4)
    rep = ATT_HEADS // ATT_KV_HEADS
    scale = ATT_HEAD_DIM ** -0.5
    key_pos = jnp.arange(s)
    gather = jax.vmap(lambda kb, ib: kb[ib])

    def one_block(i):
        start = i * Q_BLOCK
        qb = lax.dynamic_slice_in_dim(q, start, Q_BLOCK, axis=1)
        qib = lax.dynamic_slice_in_dim(q_idx, start, Q_BLOCK, axis=1)
        wib = lax.dynamic_slice_in_dim(w_idx, start, Q_BLOCK, axis=1)
        qpos = start + jnp.arange(Q_BLOCK)
        causal = key_pos[None, :] <= qpos[:, None]
        idx_s = jax.nn.relu(jnp.einsum('bqhd,bsd->bqhs', qib, k_idx))
        idx_s = jnp.einsum('bqhs,bqh->bqs', idx_s, wib).astype(jnp.float32)
        idx_s = jnp.where(causal[None], idx_s, -jnp.inf)
        _, sel = lax.top_k(idx_s, top_k)
        valid = sel <= qpos[None, :, None]
        kg = gather(k, sel)
        vg = gather(v, sel)
        qg = qb.reshape(b, Q_BLOCK, ATT_KV_HEADS, rep, ATT_HEAD_DIM)
        logits = jnp.einsum('bqgrd,bqkgd->bqgrk', qg, kg).astype(jnp.float32) * scale
        logits = jnp.where(valid[:, :, None, None, :], logits, -jnp.inf)
        p = jax.nn.softmax(logits, axis=-1).astype(v.dtype)
        o = jnp.einsum('bqgrk,bqkgd->bqgrd', p, vg)
        return o.reshape(b, Q_BLOCK, ATT_HEADS * ATT_HEAD_DIM)

    out = lax.map(one_block, jnp.arange(s // Q_BLOCK))
    return jnp.swapaxes(out, 0, 1).reshape(b, s, ATT_HEADS * ATT_HEAD_DIM)


def token_shift(p, mu):
    prev = jnp.pad(p, ((0, 0), (1, 0), (0, 0)))[:, :-1]
    return p + (prev - p) * mu


def rwkv7_time_mix(p, mu, w0, w2, a0, a2, g2, k_k, k_a, r_k, ln_g, ln_b):
    b, s, _ = p.shape
    f32 = jnp.float32
    p = token_shift(p, mu)
    r, wd, k, v, ad, gd = split_cols(p, B_COLS)
    w = -jax.nn.softplus(-(w0 + jnp.tanh(wd) @ w2)) - 0.5
    a = jax.nn.sigmoid(a0 + ad @ a2)
    g = jax.nn.sigmoid(gd) @ g2
    heads = lambda t: t.astype(f32).reshape(b, s, RWKV_HEADS, RWKV_HEAD)
    hv = lambda t: t.astype(f32).reshape(RWKV_HEADS, RWKV_HEAD)
    r, w, k, v, a = heads(r), heads(w), heads(k), heads(v), heads(a)
    kk = k * hv(k_k)
    kk = kk / jnp.maximum(jnp.sqrt(jnp.sum(kk * kk, -1, keepdims=True)), 1e-12)
    k = k * (1.0 + (a - 1.0) * hv(k_a))
    decay = jnp.exp(-jnp.exp(w))

    def step(state, inp):
        r_t, d_t, k_t, v_t, kk_t, b_t = inp
        sk = jnp.einsum('bhij,bhj->bhi', state, kk_t)
        state = (state * d_t[:, :, None, :] - sk[..., None] * b_t[:, :, None, :]
                 + v_t[..., None] * k_t[:, :, None, :])
        return state, jnp.einsum('bhij,bhj->bhi', state, r_t)

    tm = lambda t: jnp.swapaxes(t, 0, 1)
    state0 = jnp.zeros((b, RWKV_HEADS, RWKV_HEAD, RWKV_HEAD), f32)
    _, y = lax.scan(step, state0, (tm(r), tm(decay), tm(k), tm(v), tm(kk), tm(kk * a)))
    y = jnp.swapaxes(y, 0, 1)
    mean = jnp.mean(y, -1, keepdims=True)
    var = jnp.mean(jnp.square(y - mean), -1, keepdims=True)
    y = (y - mean) * lax.rsqrt(var + RWKV_GN_EPS) * hv(ln_g) + hv(ln_b)
    y = y + jnp.sum(r * k * hv(r_k), -1, keepdims=True) * v
    return y.reshape(b, s, BRANCH_WIDTH).astype(p.dtype) * g


def multiscale_pool(p, pool_w, pool_scale):
    b, s, _ = p.shape
    xf = p.astype(jnp.float32)
    csum = jnp.pad(jnp.cumsum(xf, axis=1), ((0, 0), (1, 0), (0, 0)))
    steps = jnp.arange(1, s + 1, dtype=jnp.float32)
    groups = []
    for gi, win in enumerate(POOL_WINDOWS):
        lo, hi = gi * POOL_GROUP_DIM, (gi + 1) * POOL_GROUP_DIM
        c = csum[:, :, lo:hi]
        lagged = jnp.pad(c, ((0, 0), (win, 0), (0, 0)))[:, 1:s + 1]
        mean = (c[:, 1:] - lagged) / jnp.minimum(steps, float(win))[None, :, None]
        groups.append(mean - xf[:, :, lo:hi])
    pooled = jnp.stack(groups, axis=2).astype(p.dtype)
    y = jnp.einsum('bsgc,gcd->bsgd', pooled, pool_w).reshape(b, s, BRANCH_WIDTH)
    return y * pool_scale


def retention(q, k, v, g, gn_g):
    b, s, h, dk = q.shape
    dv = v.shape[-1]
    f32 = jnp.float32
    nc = s // RET_CHUNK
    log_gamma = jnp.log(1.0 - 2.0 ** (-5.0 - jnp.arange(h, dtype=f32)))
    pos = jnp.arange(RET_CHUNK, dtype=f32)
    diff = pos[:, None] - pos[None, :]
    intra = jnp.where(diff >= 0, jnp.exp(jnp.maximum(diff, 0.0)[None] * log_gamma[:, None, None]), 0.0)
    q_decay = jnp.exp((pos + 1.0)[None, :] * log_gamma[:, None])
    k_decay = jnp.exp((RET_CHUNK - 1.0 - pos)[None, :] * log_gamma[:, None])
    c_decay = jnp.exp(RET_CHUNK * log_gamma)
    chunk = lambda t: t.astype(f32).reshape(b, nc, RET_CHUNK, h, t.shape[-1]).transpose(1, 0, 3, 2, 4)
    qc, kc, vc = chunk(q), chunk(k * dk ** -0.5), chunk(v)

    def step(state, inp):
        qj, kj, vj = inp
        sc = jnp.einsum('bhnd,bhmd->bhnm', qj, kj) * intra
        o = (jnp.einsum('bhnm,bhmv->bhnv', sc, vj)
             + jnp.einsum('bhnd,bhdv->bhnv', qj, state) * q_decay[..., None])
        state = state * c_decay[:, None, None] + jnp.einsum('bhmd,bhmv->bhdv', kj * k_decay[..., None], vj)
        return state, o

    _, o = lax.scan(step, jnp.zeros((b, h, dk, dv), f32), (qc, kc, vc))
    o = o.transpose(1, 0, 3, 2, 4).reshape(b, s, h, dv)
    mean = jnp.mean(o, -1, keepdims=True)
    var = jnp.mean(jnp.square(o - mean), -1, keepdims=True)
    o = ((o - mean) * lax.rsqrt(var + RET_GN_EPS)).reshape(b, s, h * dv) * gn_g.astype(f32)
    return jax.nn.silu(g) * o.astype(g.dtype)


def setup_inputs(seed: int = 0) -> dict:
    key = jax.random.key(seed)
    ks = jax.random.split(key, 32)
    f32 = jnp.float32
    nrm = lambda kk, shape, sc: jax.random.normal(kk, shape, f32) * sc
    L, D, W = DEPTH, D_MODEL, BRANCH_WIDTH
    return {
        'x': nrm(ks[0], (BATCH, SEQ, D), 1.0),
        'positions': jnp.tile(jnp.arange(SEQ, dtype=jnp.int32)[None, :], (BATCH, 1)),
        'attn_norm_g': 1.0 + nrm(ks[1], (L, D), 0.05),
        'w_in': nrm(ks[2], (L, D, N_IN), D ** -0.5),
        'rwkv_mu': jax.random.uniform(ks[3], (L, N_B), f32, 0.2, 0.8),
        'rwkv_w0': jnp.linspace(-6.0, -1.0, W, dtype=f32)[None, :] + nrm(ks[4], (L, W), 0.1),
        'rwkv_w2': nrm(ks[5], (L, RWKV_DECAY_LORA, W), 0.5 * RWKV_DECAY_LORA ** -0.5),
        'rwkv_a0': nrm(ks[6], (L, W), 0.1),
        'rwkv_a2': nrm(ks[7], (L, RWKV_A_LORA, W), RWKV_A_LORA ** -0.5),
        'rwkv_g2': nrm(ks[8], (L, RWKV_GATE_LORA, W), RWKV_GATE_LORA ** -0.5),
        'rwkv_k_k': 0.85 + nrm(ks[9], (L, W), 0.05),
        'rwkv_k_a': 1.0 + nrm(ks[10], (L, W), 0.05),
        'rwkv_r_k': nrm(ks[11], (L, W), 0.1),
        'rwkv_ln_g': 1.0 + nrm(ks[12], (L, W), 0.05),
        'rwkv_ln_b': nrm(ks[13], (L, W), 0.05),
        'pool_w': nrm(ks[14], (L, POOL_GROUPS, POOL_GROUP_DIM, POOL_GROUP_DIM), POOL_GROUP_DIM ** -0.5),
        'pool_scale': 1.0 + nrm(ks[15], (L, W), 0.1),
        'ret_gn_g': 1.0 + nrm(ks[16], (L, W), 0.05),
        'gate_b': nrm(ks[17], (L, N_BRANCHES, D), 0.1),
        'w_branch': nrm(ks[18], (L, N_BRANCHES, W, D), W ** -0.5),
        'w_out': nrm(ks[19], (L, D, D), D ** -0.5),
        'mlp_norm_g': 1.0 + nrm(ks[20], (L, D), 0.05),
        'mlp_up': nrm(ks[21], (L, D, D_FF), D ** -0.5),
        'mlp_down': nrm(ks[22], (L, D_FF, D), D_FF ** -0.5),
        'final_norm_g': 1.0 + nrm(ks[23], (D,), 0.05),
    }


def reference(x, positions, attn_norm_g, w_in, rwkv_mu, rwkv_w0, rwkv_w2, rwkv_a0, rwkv_a2, rwkv_g2,
              rwkv_k_k, rwkv_k_a, rwkv_r_k, rwkv_ln_g, rwkv_ln_b, pool_w, pool_scale, ret_gn_g,
              gate_b, w_branch, w_out, mlp_norm_g, mlp_up, mlp_down, final_norm_g):
    b, s, _ = x.shape
    dt = x.dtype
    inv_freq_att = ROPE_THETA ** (-jnp.arange(0, ROT_DIM, 2, dtype=jnp.float32) / ROT_DIM)
    cos_a, sin_a = cos_sin(positions, inv_freq_att, dt)
    inv_freq_ret = 1.0 / (RET_THETA ** jnp.linspace(0.0, 1.0, RET_KEY_DIM // 2, dtype=jnp.float32))
    cos_r, sin_r = cos_sin(positions, inv_freq_ret, dt)
    for l in range(DEPTH):
        h = rms_norm(x, attn_norm_g[l])
        proj = h @ w_in[l]
        pa, pb, pc, pd, pg = split_cols(proj, (N_A, N_B, N_C, N_D, N_G))
        q, k, v, qi, ki, wi = split_cols(pa, A_COLS)
        q = partial_rope(q.reshape(b, s, ATT_HEADS, ATT_HEAD_DIM), cos_a, sin_a)
        k = partial_rope(k.reshape(b, s, ATT_KV_HEADS, ATT_HEAD_DIM), cos_a, sin_a)
        v = v.reshape(b, s, ATT_KV_HEADS, ATT_HEAD_DIM)
        qi = partial_rope(qi.reshape(b, s, IDX_HEADS, IDX_HEAD_DIM), cos_a, sin_a)
        ki = partial_rope(ki[:, :, None, :], cos_a, sin_a)[:, :, 0]
        o_a = dsa_attention(q, k, v, qi, ki, wi)
        o_b = rwkv7_time_mix(pb, rwkv_mu[l], rwkv_w0[l], rwkv_w2[l], rwkv_a0[l], rwkv_a2[l], rwkv_g2[l],
                             rwkv_k_k[l], rwkv_k_a[l], rwkv_r_k[l], rwkv_ln_g[l], rwkv_ln_b[l])
        o_c = multiscale_pool(pc, pool_w[l], pool_scale[l])
        rq, rk, rv, rg = split_cols(pd, D_COLS)
        rq = rotate(rq.reshape(b, s, RET_HEADS, RET_KEY_DIM), cos_r, sin_r)
        rk = rotate(rk.reshape(b, s, RET_HEADS, RET_KEY_DIM), cos_r, sin_r)
        o_d = retention(rq, rk, rv.reshape(b, s, RET_HEADS, RET_VALUE_DIM), rg, ret_gn_g[l])
        gates = jax.nn.sigmoid(pg.reshape(b, s, N_BRANCHES, D_MODEL) + gate_b[l])
        merged = gates[:, :, 0] * (o_a @ w_branch[l, 0])
        merged = merged + gates[:, :, 1] * (o_b @ w_branch[l, 1])
        merged = merged + gates[:, :, 2] * (o_c @ w_branch[l, 2])
        merged = merged + gates[:, :, 3] * (o_d @ w_branch[l, 3])
        x = x + merged @ w_out[l]
        h2 = rms_norm(x, mlp_norm_g[l])
        x = x + jnp.square(jax.nn.relu(h2 @ mlp_up[l])) @ mlp_down[l]
    return rms_norm(x, final_norm_g)
```

```python
import functools
import math

import jax
import jax.numpy as jnp
from jax import lax
from jax.experimental import pallas as pl
from jax.experimental.pallas import tpu as pltpu

F32 = jnp.float32
BF16 = jnp.bfloat16
I32 = jnp.int32

D_MODEL = 1024
BRANCH = 512
ATT_HEADS, ATT_KV, ATT_DH, ROT_DIM = 8, 2, 64, 16
ROPE_THETA = 500000.0
IDX_HEADS, IDX_DH, INDEX_TOPK, Q_BLOCK = 4, 64, 256, 128
RW_N, RW_H = 64, 8
RW_DL, RW_AL, RW_GL = 32, 32, 96
RW_GN_EPS = 64e-5
POOL_WINDOWS = (2, 4, 8, 16)
POOL_GD = 128
RET_H, RET_DK, RET_DV = 8, 32, 64
RET_THETA = 10000.0
RET_GN_EPS = 1e-6
D_FF = 4096
NORM_EPS = 1e-5

LANES = 128
VMEM_LIMIT = 56 * 1024 * 1024

PROJ_TM = 512
DSA_KC = 512
RW_C = 64
RW_TT = 512
POOL_TS = 512
RET_C = 256
MERGE_TM = 512
MLP_TM = 512
MLP_FF = 1024

INT_MIN = -2 ** 31
NEG_BIG = -1e30

PA_W = 2304
PA_Q, PA_QI, PA_K, PA_KI, PA_V, PA_WI = 0, 1024, 1536, 1792, 1920, 2048
PB_W = 1792
PD_W = 2048


def _cp(sem, vmem=VMEM_LIMIT):
    return pltpu.CompilerParams(dimension_semantics=sem, vmem_limit_bytes=vmem)


def _dot(a, b, dims=(((1,), (0,)), ((), ())), exact=False):
    if exact:
        return lax.dot_general(a.astype(F32), b.astype(F32), dims, precision=lax.Precision.HIGHEST,
                               preferred_element_type=F32)
    return lax.dot_general(a.astype(BF16), b.astype(BF16), dims, preferred_element_type=F32)


_NN = (((1,), (0,)), ((), ()))
_NT = (((1,), (1,)), ((), ()))
_TN = (((0,), (0,)), ((), ()))


def _dot_split(x, ones_rhs, dims=_NN, left=False):
    hi = x.astype(BF16)
    lo = (x - hi.astype(F32)).astype(BF16)
    if left:
        return (lax.dot_general(ones_rhs, hi, dims, preferred_element_type=F32)
                + lax.dot_general(ones_rhs, lo, dims, preferred_element_type=F32))
    return (lax.dot_general(hi, ones_rhs, dims, preferred_element_type=F32)
            + lax.dot_general(lo, ones_rhs, dims, preferred_element_type=F32))


def _norm_matmul_kernel(x_ref, g_ref, w_ref, o_ref, h_ref):
    @pl.when(pl.program_id(1) == 0)
    def _():
        x = x_ref[...]
        ms = jnp.mean(x * x, axis=-1, keepdims=True)
        h_ref[...] = (x * lax.rsqrt(ms + NORM_EPS) * g_ref[...]).astype(BF16)

    o_ref[...] = jnp.dot(h_ref[...], w_ref[...], preferred_element_type=F32).astype(o_ref.dtype)


def _pick_tn(n):
    for tn in (1024, 896, 768, 640, 512, 384, 256, 128):
        if n % tn == 0:
            return tn
    raise ValueError(n)


def norm_matmul(x, g, w, out_dtype):
    t, d = x.shape
    n = w.shape[1]
    tm, tn = min(PROJ_TM, t), _pick_tn(n)
    return pl.pallas_call(
        _norm_matmul_kernel,
        grid=(t // tm, n // tn),
        in_specs=[pl.BlockSpec((tm, d), lambda i, j: (i, 0)),
                  pl.BlockSpec((1, d), lambda i, j: (0, 0)),
                  pl.BlockSpec((d, tn), lambda i, j: (0, j))],
        out_specs=pl.BlockSpec((tm, tn), lambda i, j: (i, j)),
        out_shape=jax.ShapeDtypeStruct((t, n), out_dtype),
        scratch_shapes=[pltpu.VMEM((tm, d), BF16)],
        compiler_params=_cp(("parallel", "arbitrary")),
        name="norm_matmul",
    )(x, g, w)


def _rope_table_kernel(pos_ref, fa_ref, fr_ref, ma_ref, mr_ref, ca_ref, sa_ref, cr_ref, sr_ref):
    pos = pos_ref[...].astype(F32)
    ang_a = pos * fa_ref[...]
    ca_ref[...] = jnp.cos(ang_a) * ma_ref[...]
    sa_ref[...] = jnp.sin(ang_a) * ma_ref[...]
    ang_r = pos * fr_ref[...]
    cr_ref[...] = jnp.cos(ang_r) * mr_ref[...]
    sr_ref[...] = jnp.sin(ang_r) * mr_ref[...]


def rope_tables(positions):
    t = positions.size
    pos = positions.reshape(t, 1)
    inv_a = ROPE_THETA ** (-jnp.arange(0, ROT_DIM, 2, dtype=F32) / ROT_DIM)
    fa = jnp.concatenate([inv_a, inv_a, jnp.zeros((LANES - ROT_DIM,), F32)])[None]
    ma = jnp.concatenate([jnp.ones((ATT_DH,), F32), jnp.zeros((LANES - ATT_DH,), F32)])[None]
    inv_r = 1.0 / (RET_THETA ** jnp.linspace(0.0, 1.0, RET_DK // 2, dtype=F32))
    fr64 = jnp.concatenate([inv_r, inv_r, jnp.zeros((RET_DK,), F32)])
    fr = jnp.concatenate([fr64, fr64])[None]
    mr64 = jnp.concatenate([jnp.ones((RET_DK,), F32), jnp.zeros((RET_DK,), F32)])
    mr = jnp.concatenate([mr64, mr64])[None]
    ts = min(1024, t)
    row = pl.BlockSpec((1, LANES), lambda i: (0, 0))
    tab = pl.BlockSpec((ts, LANES), lambda i: (i, 0))
    return pl.pallas_call(
        _rope_table_kernel,
        grid=(t // ts,),
        in_specs=[pl.BlockSpec((ts, 1), lambda i: (i, 0)), row, row, row, row],
        out_specs=[tab, tab, tab, tab],
        out_shape=[jax.ShapeDtypeStruct((t, LANES), F32)] * 4,
        compiler_params=_cp(("parallel",)),
        name="rope_tables",
    )(pos, fa, fr, ma, mr)


def _rope_apply(p, cos, sin, half_block):
    w = p.shape[-1]
    reps = w // cos.shape[-1]
    c = jnp.concatenate([cos] * reps, axis=-1) if reps > 1 else cos
    s = jnp.concatenate([sin] * reps, axis=-1) if reps > 1 else sin
    partner = pltpu.roll(p, w - half_block, axis=1)
    return p * c + partner * s


def _dsa_prep_kernel(pa_ref, cos_ref, sin_ref, q_ref, qi_ref, k_ref, ki_ref, v_ref, wi_ref):
    cos, sin = cos_ref[...], sin_ref[...]
    scale = ATT_DH ** -0.5
    q_ref[...] = (_rope_apply(pa_ref[:, PA_Q:PA_Q + 1024], cos, sin, ATT_DH) * scale).astype(BF16)
    qi_ref[...] = _rope_apply(pa_ref[:, PA_QI:PA_QI + 512], cos, sin, ATT_DH).astype(BF16)
    k_ref[...] = _rope_apply(pa_ref[:, PA_K:PA_K + 256], cos, sin, ATT_DH).astype(BF16)
    ki_ref[...] = _rope_apply(pa_ref[:, PA_KI:PA_KI + 128], cos, sin, ATT_DH).astype(BF16)
    v = pa_ref[:, PA_V:PA_V + 128]
    ones = jnp.ones((v.shape[0], ATT_DH), F32)
    v_ref[...] = jnp.concatenate([v[:, :ATT_DH], ones, v[:, ATT_DH:], ones], axis=1).astype(BF16)
    wi_ref[...] = pa_ref[:, PA_WI:PA_WI + 128]


def dsa_prep(pa, cos_a, sin_a):
    t = pa.shape[0]
    tq = min(512, t)
    blk = lambda w: pl.BlockSpec((tq, w), lambda i: (i, 0))
    return pl.pallas_call(
        _dsa_prep_kernel,
        grid=(t // tq,),
        in_specs=[blk(PA_W), blk(LANES), blk(LANES)],
        out_specs=[blk(1024), blk(512), blk(256), blk(128), blk(256), blk(128)],
        out_shape=[jax.ShapeDtypeStruct((t, 1024), BF16), jax.ShapeDtypeStruct((t, 512), BF16),
                   jax.ShapeDtypeStruct((t, 256), BF16), jax.ShapeDtypeStruct((t, 128), BF16),
                   jax.ShapeDtypeStruct((t, 256), BF16), jax.ShapeDtypeStruct((t, 128), F32)],
        compiler_params=_cp(("parallel",)),
        name="dsa_prep",
    )(pa, cos_a, sin_a)


def _dsa_kernel(q_ref, qi_ref, wi_ref, k_ref, ki_ref, v_ref, o_ref, keys_ref, *, kc, topk):
    qb = Q_BLOCK
    i = pl.program_id(1)
    n_kc = (i * qb + qb + kc - 1) // kc
    rep = ATT_HEADS // ATT_KV

    qi = qi_ref[0]
    wi = wi_ref[0]
    wib = [jnp.broadcast_to(wi[:, h:h + 1], (qb, kc)) for h in range(IDX_HEADS)]
    row = lax.broadcasted_iota(I32, (qb, kc), 0) + i * qb
    col = lax.broadcasted_iota(I32, (qb, kc), 1)

    def score_body(c, carry):
        kic = ki_ref[0, pl.ds(pl.multiple_of(c * kc, kc), kc), :]
        acc = jnp.zeros((qb, kc), F32)
        for h in range(IDX_HEADS):
            d = lax.dot_general(qi[:, h * LANES:(h + 1) * LANES], kic, _NT, preferred_element_type=F32)
            acc = acc + jnp.maximum(d, 0.0) * wib[h]
        acc = jnp.where(acc == 0.0, 0.0, acc)
        bits = pltpu.bitcast(acc, I32)
        key = jnp.where(bits < 0, bits ^ 0x7FFFFFFF, bits)
        keys_ref[c] = jnp.where(col + c * kc <= row, key, INT_MIN)
        return carry

    lax.fori_loop(0, n_kc, score_body, 0)

    def count_ge(cand):
        cb = jnp.broadcast_to(cand, (qb, LANES))

        def body(c, acc):
            blk = keys_ref[c]
            for s in range(kc // LANES):
                acc = acc + jnp.where(blk[:, s * LANES:(s + 1) * LANES] >= cb, 1.0, 0.0)
            return acc

        acc = lax.fori_loop(0, n_kc, body, jnp.zeros((qb, LANES), F32))
        return jnp.sum(acc, axis=1, keepdims=True)

    kf = float(topk)
    cnt0 = count_ge(jnp.zeros((qb, 1), I32))
    ok0 = cnt0 >= kf
    ans = jnp.where(ok0, 0, INT_MIN).astype(I32)
    cnt_ans = jnp.where(ok0, cnt0, 0.0)

    def bis_body(it, carry):
        ans, cnt_ans = carry
        cand = ans | jnp.left_shift(jnp.int32(1), 30 - it)
        cnt = count_ge(cand)
        ok = cnt >= kf
        return jnp.where(ok, cand, ans), jnp.where(ok, cnt, cnt_ans)

    ans, cnt_ans = lax.fori_loop(0, 31, bis_body, (ans, cnt_ans))

    tie = (ans > INT_MIN) & (cnt_ans > kf)
    any_tie = jnp.max(jnp.where(tie, 1.0, 0.0)) > 0.0

    @pl.when(any_tie)
    def _():
        need = kf - count_ge(ans + 1)
        ansb = jnp.broadcast_to(ans, (qb, kc))
        tieb = jnp.broadcast_to(jnp.where(tie, 1.0, 0.0), (qb, kc)) > 0.0
        upper = (lax.broadcasted_iota(I32, (kc, kc), 0) <= lax.broadcasted_iota(I32, (kc, kc), 1))
        upper = jnp.where(upper, 1.0, 0.0).astype(BF16)

        def tie_body(c, run):
            blk = keys_ref[c]
            eq = (blk == ansb) & tieb
            eqf = jnp.where(eq, 1.0, 0.0)
            pref = jnp.dot(eqf.astype(BF16), upper, preferred_element_type=F32) + run
            keys_ref[c] = jnp.where(eq & (pref > need), INT_MIN, blk)
            return run + jnp.sum(eqf, axis=1, keepdims=True)

        lax.fori_loop(0, n_kc, tie_body, jnp.zeros((qb, 1), F32))

    thr = jnp.broadcast_to(jnp.maximum(ans, INT_MIN + 1), (qb, kc))

    q = q_ref[0]
    qg = [jnp.concatenate([q[:, (g * rep + r) * LANES:(g * rep + r + 1) * LANES] for r in range(rep)], axis=0)
          for g in range(ATT_KV)]

    def att_body(c, carry):
        rows = pl.ds(pl.multiple_of(c * kc, kc), kc)
        bias = jnp.where(keys_ref[c] >= thr, 0.0, NEG_BIG)
        kch = k_ref[0, rows, :]
        vch = v_ref[0, rows, :]
        out = []
        for g in range(ATT_KV):
            m_old, acc = carry[2 * g], carry[2 * g + 1]
            s = lax.dot_general(qg[g], kch[:, g * LANES:(g + 1) * LANES], _NT, preferred_element_type=F32)
            s = s.reshape(rep, qb, kc) + bias[None]
            m_new = jnp.maximum(m_old, jnp.max(s, axis=2, keepdims=True))
            alpha = jnp.exp(m_old - m_new)
            p = jnp.exp(s - m_new).astype(BF16).reshape(rep * qb, kc)
            pv = jnp.dot(p, vch[:, g * LANES:(g + 1) * LANES], preferred_element_type=F32)
            acc = acc * alpha.reshape(rep * qb, 1) + pv
            out += [m_new, acc]
        return tuple(out)

    init = []
    for g in range(ATT_KV):
        init += [jnp.full((rep, qb, 1), NEG_BIG, F32), jnp.zeros((rep * qb, LANES), F32)]
    res = lax.fori_loop(0, n_kc, att_body, tuple(init))

    outs = []
    for g in range(ATT_KV):
        acc = res[2 * g + 1]
        o = acc[:, :ATT_DH] / acc[:, ATT_DH:]
        outs += [o[r * qb:(r + 1) * qb] for r in range(rep)]
    o_ref[0] = jnp.concatenate(outs, axis=1).astype(o_ref.dtype)


def dsa_attention(q, qi, wi, k, ki, v, b, s):
    kc = min(DSA_KC, s)
    topk = min(INDEX_TOPK, s // 4)
    r3 = lambda a: a.reshape(b, s, a.shape[-1])
    qblk = lambda w: pl.BlockSpec((1, Q_BLOCK, w), lambda bi, i: (bi, i, 0))
    full = lambda w: pl.BlockSpec((1, s, w), lambda bi, i: (bi, 0, 0))
    out = pl.pallas_call(
        functools.partial(_dsa_kernel, kc=kc, topk=topk),
        grid=(b, s // Q_BLOCK),
        in_specs=[qblk(1024), qblk(512), qblk(128), full(256), full(128), full(256)],
        out_specs=qblk(BRANCH),
        out_shape=jax.ShapeDtypeStruct((b, s, BRANCH), BF16),
        scratch_shapes=[pltpu.VMEM((s // kc, Q_BLOCK, kc), I32)],
        compiler_params=_cp(("parallel", "arbitrary")),
        name="dsa_attention",
    )(r3(q), r3(qi), r3(wi), r3(k), r3(ki), r3(v))
    return out.reshape(b * s, BRANCH)


def _tri_inverse(n_mat, eye, blk_masks, exact):
    n8 = jnp.where(blk_masks[8], n_mat, 0.0)
    n2 = _dot(n8, n8, exact=exact)
    n4 = _dot(n2, n2, exact=exact)
    t = _dot(_dot(eye - n8, eye + n2, exact=exact), eye + n4, exact=exact)
    sz = 16
    while sz <= n_mat.shape[0]:
        off = jnp.where(blk_masks[sz] & jnp.logical_not(blk_masks[sz // 2]), n_mat, 0.0)
        t = t - _dot(_dot(t, off, exact=exact), t, exact=exact)
        sz *= 2
    return t


def _rwkv_kernel(pb_ref, mu_ref, vec_ref, lora_ref, bones_ref, o_ref,
                 carry_ref, state_ref, r_s, k_s, v_s, kk_s, b_s, ld_s, y_s, *, exact):
    tt = pb_ref.shape[1]
    c_len = RW_C
    t_idx = pl.program_id(1)

    @pl.when(t_idx == 0)
    def _():
        carry_ref[...] = jnp.zeros_like(carry_ref)
        state_ref[...] = jnp.zeros_like(state_ref)

    p = pb_ref[0]
    prev = pltpu.roll(p, 1, axis=0)
    first = lax.broadcasted_iota(I32, p.shape, 0) == 0
    prev = jnp.where(first, jnp.broadcast_to(carry_ref[7:8, :], p.shape), prev)
    carry_ref[...] = p[tt - 8:, :]
    p = p + (prev - p) * mu_ref[...]

    w0, a0, k_k, k_a = vec_ref[0:1, :], vec_ref[1:2, :], vec_ref[2:3, :], vec_ref[3:4, :]
    r_k, ln_g, ln_b = vec_ref[4:5, :], vec_ref[5:6, :], vec_ref[6:7, :]
    bones = bones_ref[...]

    r = p[:, 0:512]
    k = p[:, 512:1024]
    v = p[:, 1024:1536]
    lo = p[:, 1536:1792]
    wl = _dot(jnp.tanh(lo), lora_ref[0])
    al = _dot(lo, lora_ref[1])
    g = _dot(jax.nn.sigmoid(lo), lora_ref[2])
    z = -(w0 + wl)
    w = -(jnp.maximum(z, 0.0) + jnp.log(1.0 + jnp.exp(-jnp.abs(z)))) - 0.5
    a = jax.nn.sigmoid(a0 + al)
    kk = k * k_k
    ssq = _dot_split(kk * kk, bones)
    kk = kk / jnp.maximum(jnp.sqrt(ssq), 1e-12)
    k = k * (1.0 + (a - 1.0) * k_a)
    r_s[...] = r
    k_s[...] = k
    v_s[...] = v
    kk_s[...] = kk
    b_s[...] = kk * a
    ld_s[...] = -jnp.exp(w)

    ii = lax.broadcasted_iota(I32, (c_len, c_len), 0)
    jj = lax.broadcasted_iota(I32, (c_len, c_len), 1)
    strict, incl = ii > jj, ii >= jj
    eye = jnp.where(ii == jj, 1.0, 0.0)
    tri = jnp.where(incl, 1.0, 0.0).astype(BF16)
    blk_masks = {1 << sh: (ii >> sh) == (jj >> sh) for sh in (3, 4, 5, 6)}

    def chunk_body(c, carry):
        rows = pl.ds(pl.multiple_of(c * c_len, c_len), c_len)
        ld = ld_s[rows, :]
        cum = _dot_split(ld, tri, left=True)
        cum_prev = cum - ld
        cum_end = cum[c_len - 1:c_len, :]
        e_cum, e_prev = jnp.exp(cum), jnp.exp(cum_prev)
        e_inv, e_end = jnp.exp(-cum), jnp.exp(cum_end - cum)
        kc_, bc_ = k_s[rows, :], b_s[rows, :]
        kkt = kk_s[rows, :] * e_prev
        rt = r_s[rows, :] * e_cum
        kd, bd = kc_ * e_inv, bc_ * e_inv
        kh, bh = kc_ * e_end, bc_ * e_end
        vc = v_s[rows, :]
        e_last = jnp.exp(cum_end)
        for h in range(RW_H):
            sl = slice(h * RW_N, (h + 1) * RW_N)
            big = _dot(jnp.concatenate([kkt[:, sl], rt[:, sl]], axis=0),
                       jnp.concatenate([bd[:, sl], kd[:, sl]], axis=0), _NT, exact=exact)
            m_b = jnp.where(strict, big[:c_len, :c_len], 0.0)
            m_k = jnp.where(strict, big[:c_len, c_len:], 0.0)
            p_b = jnp.where(incl, big[c_len:, :c_len], 0.0)
            p_k = jnp.where(incl, big[c_len:, c_len:], 0.0)
            t_inv = _tri_inverse(m_b, eye, blk_masks, exact)
            vh = vc[:, sl]
            mkv = _dot(m_k, vh, exact=exact)
            wu = _dot(t_inv, jnp.concatenate([kkt[:, sl], mkv], axis=1), exact=exact)
            pwu = _dot(p_b, wu, exact=exact)
            y0 = _dot(p_k, vh, exact=exact) - pwu[:, RW_N:]
            gm = rt[:, sl] - pwu[:, :RW_N]
            bw = _dot(bh[:, sl], wu[:, :RW_N], _TN, exact=exact)
            psi_t = _dot(jnp.concatenate([vh, wu[:, RW_N:]], axis=0),
                         jnp.concatenate([kh[:, sl], -bh[:, sl]], axis=0), _TN, exact=exact)
            st = state_ref[h]
            y_s[rows, sl] = y0 + _dot(gm, st, _NT, exact=True)
            state_ref[h] = st * e_last[:, sl] - _dot(st, bw, _NT, exact=True) + psi_t
        return carry

    lax.fori_loop(0, tt // c_len, chunk_body, 0)

    y = y_s[...]
    r, k, v = r_s[...], k_s[...], v_s[...]
    mean = _dot_split(y, bones) * (1.0 / RW_N)
    yc = y - mean
    var = _dot_split(yc * yc, bones) * (1.0 / RW_N)
    y = yc * lax.rsqrt(var + RW_GN_EPS) * ln_g + ln_b
    y = y + _dot_split(r * k * r_k, bones) * v
    o_ref[0] = (y * g).astype(o_ref.dtype)


def rwkv_mix(pb, mu, vecs, lora, b, s, exact=False):
    tt = min(RW_TT, s)
    bones = (jnp.arange(BRANCH)[:, None] // RW_N == jnp.arange(BRANCH)[None, :] // RW_N).astype(BF16)
    const = lambda shape: pl.BlockSpec(shape, lambda bi, ti: (0,) * len(shape))
    sq = lambda: pltpu.VMEM((tt, BRANCH), F32)
    out = pl.pallas_call(
        functools.partial(_rwkv_kernel, exact=exact),
        grid=(b, s // tt),
        in_specs=[pl.BlockSpec((1, tt, PB_W), lambda bi, ti: (bi, ti, 0)),
                  const((1, PB_W)), const((8, BRANCH)), const((3, 256, BRANCH)), const((BRANCH, BRANCH))],
        out_specs=pl.BlockSpec((1, tt, BRANCH), lambda bi, ti: (bi, ti, 0)),
        out_shape=jax.ShapeDtypeStruct((b, s, BRANCH), BF16),
        scratch_shapes=[pltpu.VMEM((8, PB_W), F32), pltpu.VMEM((RW_H, RW_N, RW_N), F32),
                        sq(), sq(), sq(), sq(), sq(), sq(), sq()],
        compiler_params=_cp(("parallel", "arbitrary")),
        name="rwkv_mix",
    )(pb.reshape(b, s, PB_W), mu, vecs, lora, bones)
    return out.reshape(b * s, BRANCH)


def _pool_kernel(pc_ref, w_ref, sc_ref, o_ref, carry_ref):
    ts = pc_ref.shape[1]
    t_idx = pl.program_id(1)

    @pl.when(t_idx == 0)
    def _():
        carry_ref[...] = jnp.zeros_like(carry_ref)

    x = pc_ref[0]
    ext = jnp.concatenate([carry_ref[...], x], axis=0)
    carry_ref[...] = x[ts - 16:, :]
    n = ts + 16
    s2 = ext + pltpu.roll(ext, 1, axis=0)
    s4 = s2[:, 128:] + pltpu.roll(s2[:, 128:], 2, axis=0)
    s8 = s4[:, 128:] + pltpu.roll(s4[:, 128:], 4, axis=0)
    s16 = s8[:, 128:] + pltpu.roll(s8[:, 128:], 8, axis=0)
    sums = (s2[16:, 0:128], s4[16:, 0:128], s8[16:, 0:128], s16[16:, 0:128])
    step = (lax.broadcasted_iota(I32, (ts, POOL_GD), 0) + t_idx * ts + 1).astype(F32)
    outs = []
    for gi, win in enumerate(POOL_WINDOWS):
        mean = sums[gi] / jnp.minimum(step, float(win))
        pooled = mean - x[:, gi * POOL_GD:(gi + 1) * POOL_GD]
        outs.append(jnp.dot(pooled.astype(BF16), w_ref[gi], preferred_element_type=F32))
    del n
    o_ref[0] = (jnp.concatenate(outs, axis=1) * sc_ref[...]).astype(o_ref.dtype)


def pool_mix(pc, pool_w, pool_scale, b, s):
    ts = min(POOL_TS, s)
    out = pl.pallas_call(
        _pool_kernel,
        grid=(b, s // ts),
        in_specs=[pl.BlockSpec((1, ts, BRANCH), lambda bi, ti: (bi, ti, 0)),
                  pl.BlockSpec((4, POOL_GD, POOL_GD), lambda bi, ti: (0, 0, 0)),
                  pl.BlockSpec((1, BRANCH), lambda bi, ti: (0, 0))],
        out_specs=pl.BlockSpec((1, ts, BRANCH), lambda bi, ti: (bi, ti, 0)),
        out_shape=jax.ShapeDtypeStruct((b, s, BRANCH), BF16),
        scratch_shapes=[pltpu.VMEM((16, BRANCH), F32)],
        compiler_params=_cp(("parallel", "arbitrary")),
        name="pool_mix",
    )(pc.reshape(b, s, BRANCH), pool_w, pool_scale)
    return out.reshape(b * s, BRANCH)


def _ret_kernel(pd_ref, cos_ref, sin_ref, gn_ref, bones_ref, o_ref, state_ref, o_s):
    c_len = pd_ref.shape[1]
    t_idx = pl.program_id(1)

    @pl.when(t_idx == 0)
    def _():
        state_ref[...] = jnp.zeros_like(state_ref)

    cos, sin = cos_ref[0], sin_ref[0]
    q = _rope_apply(pd_ref[0, :, 0:512], cos, sin, RET_DK)
    k = _rope_apply(pd_ref[0, :, 512:1024], cos, sin, RET_DK) * (RET_DK ** -0.5)
    v = pd_ref[0, :, 1024:1536]
    gate = pd_ref[0, :, 1536:2048]

    ii = lax.broadcasted_iota(I32, (c_len, c_len), 0)
    jj = lax.broadcasted_iota(I32, (c_len, c_len), 1)
    diff = (ii - jj).astype(F32)
    pos = lax.broadcasted_iota(I32, (c_len, 1), 0).astype(F32)
    for h in range(RET_H):
        log_gamma = math.log(1.0 - 2.0 ** (-5.0 - h))
        intra = jnp.where(diff >= 0, jnp.exp(jnp.maximum(diff, 0.0) * log_gamma), 0.0)
        q_decay = jnp.exp((pos + 1.0) * log_gamma)
        k_decay = jnp.exp((c_len - 1.0 - pos) * log_gamma)
        c_decay = math.exp(c_len * log_gamma)
        sl = slice(h * 64, (h + 1) * 64)
        qh, kh, vh = q[:, sl], k[:, sl], v[:, sl]
        st = state_ref[h]
        sc = _dot(qh, kh, _NT) * intra
        o = _dot(sc, vh) + _dot(qh, st) * q_decay
        state_ref[h] = st * c_decay + _dot(kh * k_decay, vh, _TN)
        o_s[:, sl] = o

    o = o_s[...]
    bones = bones_ref[...]
    mean = _dot_split(o, bones) * (1.0 / RET_DV)
    oc = o - mean
    var = _dot_split(oc * oc, bones) * (1.0 / RET_DV)
    o = oc * lax.rsqrt(var + RET_GN_EPS) * gn_ref[...]
    o_ref[0] = (jax.nn.silu(gate) * o).astype(o_ref.dtype)


def ret_mix(pd, cos_r, sin_r, gn_g, b, s):
    c_len = min(RET_C, s)
    bones = (jnp.arange(BRANCH)[:, None] // RET_DV == jnp.arange(BRANCH)[None, :] // RET_DV).astype(BF16)
    out = pl.pallas_call(
        _ret_kernel,
        grid=(b, s // c_len),
        in_specs=[pl.BlockSpec((1, c_len, PD_W), lambda bi, ti: (bi, ti, 0)),
                  pl.BlockSpec((1, c_len, LANES), lambda bi, ti: (bi, ti, 0)),
                  pl.BlockSpec((1, c_len, LANES), lambda bi, ti: (bi, ti, 0)),
                  pl.BlockSpec((1, BRANCH), lambda bi, ti: (0, 0)),
                  pl.BlockSpec((BRANCH, BRANCH), lambda bi, ti: (0, 0))],
        out_specs=pl.BlockSpec((1, c_len, BRANCH), lambda bi, ti: (bi, ti, 0)),
        out_shape=jax.ShapeDtypeStruct((b, s, BRANCH), BF16),
        scratch_shapes=[pltpu.VMEM((RET_H, 64, RET_DV), F32), pltpu.VMEM((c_len, BRANCH), F32)],
        compiler_params=_cp(("parallel", "arbitrary")),
        name="ret_mix",
    )(pd.reshape(b, s, PD_W), cos_r.reshape(b, s, LANES), sin_r.reshape(b, s, LANES), gn_g, bones)
    return out.reshape(b * s, BRANCH)


def _merge_kernel(x_ref, pg_ref, oa_ref, ob_ref, oc_ref, od_ref, gb_ref, wb_ref, wo_ref, o_ref):
    merged = None
    for i, br in enumerate((oa_ref, ob_ref, oc_ref, od_ref)):
        gate = jax.nn.sigmoid(pg_ref[:, i * D_MODEL:(i + 1) * D_MODEL].astype(F32) + gb_ref[i:i + 1, :])
        term = gate * jnp.dot(br[...], wb_ref[i], preferred_element_type=F32)
        merged = term if merged is None else merged + term
    o_ref[...] = x_ref[...] + jnp.dot(merged.astype(BF16), wo_ref[...], preferred_element_type=F32)


def merge_out(x, pg, o_a, o_b, o_c, o_d, gate_b, w_branch, w_out):
    t = x.shape[0]
    tm = min(MERGE_TM, t)
    row = lambda w: pl.BlockSpec((tm, w), lambda i: (i, 0))
    return pl.pallas_call(
        _merge_kernel,
        grid=(t // tm,),
        in_specs=[row(D_MODEL), row(4 * D_MODEL), row(BRANCH), row(BRANCH), row(BRANCH), row(BRANCH),
                  pl.BlockSpec((4, D_MODEL), lambda i: (0, 0)),
                  pl.BlockSpec((4, BRANCH, D_MODEL), lambda i: (0, 0, 0)),
                  pl.BlockSpec((D_MODEL, D_MODEL), lambda i: (0, 0))],
        out_specs=row(D_MODEL),
        out_shape=jax.ShapeDtypeStruct((t, D_MODEL), F32),
        compiler_params=_cp(("parallel",)),
        name="merge_out",
    )(x, pg, o_a, o_b, o_c, o_d, gate_b, w_branch, w_out)


def _mlp_kernel(x_ref, g_ref, up_ref, down_ref, fg_ref, o_ref, h_ref, acc_ref, *, final_norm):
    j = pl.program_id(1)

    @pl.when(j == 0)
    def _():
        x = x_ref[...]
        ms = jnp.mean(x * x, axis=-1, keepdims=True)
        h_ref[...] = (x * lax.rsqrt(ms + NORM_EPS) * g_ref[...]).astype(BF16)
        acc_ref[...] = x

    u = jnp.dot(h_ref[...], up_ref[...], preferred_element_type=F32)
    u = jnp.square(jnp.maximum(u, 0.0))
    acc_ref[...] += jnp.dot(u.astype(BF16), down_ref[...], preferred_element_type=F32)

    @pl.when(j == pl.num_programs(1) - 1)
    def _():
        y = acc_ref[...]
        if final_norm:
            ms = jnp.mean(y * y, axis=-1, keepdims=True)
            y = y * lax.rsqrt(ms + NORM_EPS) * fg_ref[...]
        o_ref[...] = y


def mlp_block(x, g, up, down, final_g, final_norm):
    t = x.shape[0]
    tm = min(MLP_TM, t)
    return pl.pallas_call(
        functools.partial(_mlp_kernel, final_norm=final_norm),
        grid=(t // tm, D_FF // MLP_FF),
        in_specs=[pl.BlockSpec((tm, D_MODEL), lambda i, j: (i, 0)),
                  pl.BlockSpec((1, D_MODEL), lambda i, j: (0, 0)),
                  pl.BlockSpec((D_MODEL, MLP_FF), lambda i, j: (0, j)),
                  pl.BlockSpec((MLP_FF, D_MODEL), lambda i, j: (j, 0)),
                  pl.BlockSpec((1, D_MODEL), lambda i, j: (0, 0))],
        out_specs=pl.BlockSpec((tm, D_MODEL), lambda i, j: (i, 0)),
        out_shape=jax.ShapeDtypeStruct((t, D_MODEL), F32),
        scratch_shapes=[pltpu.VMEM((tm, D_MODEL), BF16), pltpu.VMEM((tm, D_MODEL), F32)],
        compiler_params=_cp(("parallel", "arbitrary")),
        name="mlp_block",
    )(x, g, up, down, final_g)


def _partner_cols(w, heads, dh, rot):
    lead = w.shape[:-1]
    w3 = w.reshape(*lead, heads, dh)
    half = rot // 2
    part = jnp.concatenate([-w3[..., half:rot], w3[..., :half], jnp.zeros_like(w3[..., rot:])], axis=-1)
    return w3, part


def _interleave(w, heads, dh, rot):
    w3, part = _partner_cols(w, heads, dh, rot)
    return jnp.concatenate([w3, part], axis=-1).reshape(*w.shape[:-1], heads * 2 * dh)


def _prep_weights(w_in, rwkv_mu):
    n_a = 512 + 128 + 128 + 256 + 64 + 4
    n_b = 1696
    off_b, off_c, off_d = n_a, n_a + n_b, n_a + n_b + 512
    off_g = off_d + 1536
    wa = w_in[..., :n_a]
    q, k, v = wa[..., 0:512], wa[..., 512:640], wa[..., 640:768]
    qi, ki, wi = wa[..., 768:1024], wa[..., 1024:1088], wa[..., 1088:1092]
    pad = lambda w, n: jnp.pad(w, [(0, 0)] * (w.ndim - 1) + [(0, n - w.shape[-1])])
    w_a = jnp.concatenate([_interleave(q, ATT_HEADS, ATT_DH, ROT_DIM), _interleave(qi, IDX_HEADS, IDX_DH, ROT_DIM),
                           _interleave(k, ATT_KV, ATT_DH, ROT_DIM), _interleave(ki, 1, IDX_DH, ROT_DIM),
                           v, pad(wi, 128)], axis=-1)
    w_a = pad(w_a, PA_W)
    wb = w_in[..., off_b:off_c]
    perm = lambda z: jnp.concatenate([z[..., 0:512], z[..., 544:1056], z[..., 1056:1568],
                                      z[..., 512:544], z[..., 1568:1600], z[..., 1600:1696]], axis=-1)
    w_b = pad(perm(wb), PB_W)
    mu = pad(perm(rwkv_mu), PB_W)
    w_c = w_in[..., off_c:off_d]
    wd = w_in[..., off_d:off_g]
    rq, rk, rv, rg = wd[..., 0:256], wd[..., 256:512], wd[..., 512:1024], wd[..., 1024:1536]
    w_d = jnp.concatenate([_interleave(rq, RET_H, RET_DK, RET_DK), _interleave(rk, RET_H, RET_DK, RET_DK), rv, rg],
                          axis=-1)
    w_g = w_in[..., off_g:]
    cast = lambda w: w.astype(BF16)
    return cast(w_a), cast(w_b), cast(w_c), cast(w_d), cast(w_g), mu


def kernel(x, positions, attn_norm_g, w_in, rwkv_mu, rwkv_w0, rwkv_w2, rwkv_a0, rwkv_a2, rwkv_g2,
           rwkv_k_k, rwkv_k_a, rwkv_r_k, rwkv_ln_g, rwkv_ln_b, pool_w, pool_scale, ret_gn_g,
           gate_b, w_branch, w_out, mlp_norm_g, mlp_up, mlp_down, final_norm_g):
    b, s, d = x.shape
    depth = w_in.shape[0]
    t = b * s
    w_a, w_b, w_c, w_d, w_g, mu = _prep_weights(w_in, rwkv_mu)
    zeros = jnp.zeros_like(rwkv_w0)
    vecs = jnp.stack([rwkv_w0, rwkv_a0, rwkv_k_k, rwkv_k_a, rwkv_r_k, rwkv_ln_g, rwkv_ln_b, zeros], axis=1)
    lora = jnp.zeros((depth, 3, 256, BRANCH), F32)
    lora = lora.at[:, 0, 0:RW_DL].set(rwkv_w2)
    lora = lora.at[:, 1, RW_DL:RW_DL + RW_AL].set(rwkv_a2)
    lora = lora.at[:, 2, RW_DL + RW_AL:RW_DL + RW_AL + RW_GL].set(rwkv_g2)
    lora = lora.astype(BF16)
    pool_w16, w_branch16, w_out16 = pool_w.astype(BF16), w_branch.astype(BF16), w_out.astype(BF16)
    up16, down16 = mlp_up.astype(BF16), mlp_down.astype(BF16)

    cos_a, sin_a, cos_r, sin_r = rope_tables(positions)
    xf = x.reshape(t, d)
    for l in range(depth):
        g = attn_norm_g[l][None]
        pa = norm_matmul(xf, g, w_a[l], F32)
        pb = norm_matmul(xf, g, w_b[l], F32)
        pc = norm_matmul(xf, g, w_c[l], F32)
        pd = norm_matmul(xf, g, w_d[l], F32)
        pg = norm_matmul(xf, g, w_g[l], BF16)
        q, qi, k, ki, v, wi = dsa_prep(pa, cos_a, sin_a)
        o_a = dsa_attention(q, qi, wi, k, ki, v, b, s)
        o_b = rwkv_mix(pb, mu[l][None], vecs[l], lora[l], b, s)
        o_c = pool_mix(pc, pool_w16[l], pool_scale[l][None], b, s)
        o_d = ret_mix(pd, cos_r, sin_r, ret_gn_g[l][None], b, s)
        xf = merge_out(xf, pg, o_a, o_b, o_c, o_d, gate_b[l], w_branch16[l], w_out16[l])
        xf = mlp_block(xf, mlp_norm_g[l][None], up16[l], down16[l], final_norm_g[None], l == depth - 1)
    return xf.reshape(b, s, d)
```

```python
import functools
import math

import jax
import jax.numpy as jnp
from jax import lax
from jax.experimental import pallas as pl
from jax.experimental.pallas import tpu as pltpu

F32 = jnp.float32
BF16 = jnp.bfloat16
I32 = jnp.int32

D_MODEL = 1024
BRANCH = 512
ATT_HEADS, ATT_KV, ATT_DH, ROT_DIM = 8, 2, 64, 16
ROPE_THETA = 500000.0
IDX_HEADS, IDX_DH, INDEX_TOPK, Q_BLOCK = 4, 64, 256, 128
RW_N, RW_H = 64, 8
RW_DL, RW_AL, RW_GL = 32, 32, 96
RW_GN_EPS = 64e-5
POOL_WINDOWS = (2, 4, 8, 16)
POOL_GD = 128
RET_H, RET_DK, RET_DV = 8, 32, 64
RET_THETA = 10000.0
RET_GN_EPS = 1e-6
D_FF = 4096
NORM_EPS = 1e-5

LANES = 128
VMEM_LIMIT = 56 * 1024 * 1024

PROJ_TM = 512
DSA_KC = 512
RW_C = 64
RW_TT = 512
RW_PAR = 2
POOL_TS = 512
RET_C = 256
MERGE_TM = 512
MLP_TM = 512
MLP_FF = 1024

INT_MIN = -2 ** 31
I16 = jnp.int16
HALF16 = 2 ** 15
LOG2E = 1.4426950408889634
NEG_BIG = -1e30

PA_W = 2304
PA_Q, PA_QI, PA_K, PA_KI, PA_V, PA_WI = 0, 1024, 1536, 1792, 1920, 2048
PB_W = 1792
PD_W = 2048


def _cp(sem, vmem=VMEM_LIMIT):
    return pltpu.CompilerParams(dimension_semantics=sem, vmem_limit_bytes=vmem)


def _dot(a, b, dims=(((1,), (0,)), ((), ())), exact=False):
    if exact:
        return lax.dot_general(a.astype(F32), b.astype(F32), dims, precision=lax.Precision.HIGHEST,
                               preferred_element_type=F32)
    return lax.dot_general(a.astype(BF16), b.astype(BF16), dims, preferred_element_type=F32)


_NN = (((1,), (0,)), ((), ()))
_NT = (((1,), (1,)), ((), ()))
_TN = (((0,), (0,)), ((), ()))


def _dot_split(x, ones_rhs, dims=_NN, left=False):
    hi = x.astype(BF16)
    lo = (x - hi.astype(F32)).astype(BF16)
    if left:
        return (lax.dot_general(ones_rhs, hi, dims, preferred_element_type=F32)
                + lax.dot_general(ones_rhs, lo, dims, preferred_element_type=F32))
    return (lax.dot_general(hi, ones_rhs, dims, preferred_element_type=F32)
            + lax.dot_general(lo, ones_rhs, dims, preferred_element_type=F32))


def _norm_matmul_kernel(x_ref, g_ref, w_ref, o_ref, h_ref):
    @pl.when(pl.program_id(1) == 0)
    def _():
        x = x_ref[...]
        ms = jnp.mean(x * x, axis=-1, keepdims=True)
        h_ref[...] = (x * lax.rsqrt(ms + NORM_EPS) * g_ref[...]).astype(BF16)

    o_ref[...] = jnp.dot(h_ref[...], w_ref[...], preferred_element_type=F32).astype(o_ref.dtype)


def _pick_tn(n):
    for tn in (1024, 896, 768, 640, 512, 384, 256, 128):
        if n % tn == 0:
            return tn
    raise ValueError(n)


def norm_matmul(x, g, w, out_dtype):
    t, d = x.shape
    n = w.shape[1]
    tm, tn = min(PROJ_TM, t), _pick_tn(n)
    return pl.pallas_call(
        _norm_matmul_kernel,
        grid=(t // tm, n // tn),
        in_specs=[pl.BlockSpec((tm, d), lambda i, j: (i, 0)),
                  pl.BlockSpec((1, d), lambda i, j: (0, 0)),
                  pl.BlockSpec((d, tn), lambda i, j: (0, j))],
        out_specs=pl.BlockSpec((tm, tn), lambda i, j: (i, j)),
        out_shape=jax.ShapeDtypeStruct((t, n), out_dtype),
        scratch_shapes=[pltpu.VMEM((tm, d), BF16)],
        compiler_params=_cp(("parallel", "arbitrary")),
        name="norm_matmul",
    )(x, g, w)


def _rope_table_kernel(pos_ref, fa_ref, fr_ref, ma_ref, mr_ref, ca_ref, sa_ref, cr_ref, sr_ref):
    pos = pos_ref[...].astype(F32)
    ang_a = pos * fa_ref[...]
    ca_ref[...] = jnp.cos(ang_a) * ma_ref[...]
    sa_ref[...] = jnp.sin(ang_a) * ma_ref[...]
    ang_r = pos * fr_ref[...]
    cr_ref[...] = jnp.cos(ang_r) * mr_ref[...]
    sr_ref[...] = jnp.sin(ang_r) * mr_ref[...]


def rope_tables(positions):
    t = positions.size
    pos = positions.reshape(t, 1)
    inv_a = ROPE_THETA ** (-jnp.arange(0, ROT_DIM, 2, dtype=F32) / ROT_DIM)
    fa = jnp.concatenate([inv_a, inv_a, jnp.zeros((LANES - ROT_DIM,), F32)])[None]
    ma = jnp.concatenate([jnp.ones((ATT_DH,), F32), jnp.zeros((LANES - ATT_DH,), F32)])[None]
    inv_r = 1.0 / (RET_THETA ** jnp.linspace(0.0, 1.0, RET_DK // 2, dtype=F32))
    fr64 = jnp.concatenate([inv_r, inv_r, jnp.zeros((RET_DK,), F32)])
    fr = jnp.concatenate([fr64, fr64])[None]
    mr64 = jnp.concatenate([jnp.ones((RET_DK,), F32), jnp.zeros((RET_DK,), F32)])
    mr = jnp.concatenate([mr64, mr64])[None]
    ts = min(1024, t)
    row = pl.BlockSpec((1, LANES), lambda i: (0, 0))
    tab = pl.BlockSpec((ts, LANES), lambda i: (i, 0))
    return pl.pallas_call(
        _rope_table_kernel,
        grid=(t // ts,),
        in_specs=[pl.BlockSpec((ts, 1), lambda i: (i, 0)), row, row, row, row],
        out_specs=[tab, tab, tab, tab],
        out_shape=[jax.ShapeDtypeStruct((t, LANES), F32)] * 4,
        compiler_params=_cp(("parallel",)),
        name="rope_tables",
    )(pos, fa, fr, ma, mr)


def _rope_apply(p, cos, sin, half_block):
    w = p.shape[-1]
    reps = w // cos.shape[-1]
    c = jnp.concatenate([cos] * reps, axis=-1) if reps > 1 else cos
    s = jnp.concatenate([sin] * reps, axis=-1) if reps > 1 else sin
    partner = pltpu.roll(p, w - half_block, axis=1)
    return p * c + partner * s


def _dsa_prep_kernel(pa_ref, cos_ref, sin_ref, q_ref, qi_ref, k_ref, ki_ref, vt_ref, wi_ref):
    cos, sin = cos_ref[0], sin_ref[0]
    q_scale = ATT_DH ** -0.5 * LOG2E
    q_ref[0] = (_rope_apply(pa_ref[0, :, PA_Q:PA_Q + 1024], cos, sin, ATT_DH) * q_scale).astype(BF16)
    qi_ref[0] = _rope_apply(pa_ref[0, :, PA_QI:PA_QI + 512], cos, sin, ATT_DH).astype(BF16)
    k_ref[0] = _rope_apply(pa_ref[0, :, PA_K:PA_K + 256], cos, sin, ATT_DH).astype(BF16)
    ki_ref[0] = _rope_apply(pa_ref[0, :, PA_KI:PA_KI + 128], cos, sin, ATT_DH).astype(BF16)
    vt = pa_ref[0, :, PA_V:PA_V + 128].T
    ones = jnp.ones((ATT_DH, vt.shape[1]), F32)
    vt_ref[0] = jnp.concatenate([vt[:ATT_DH], ones, vt[ATT_DH:], ones], axis=0).astype(BF16)
    wi_ref[0] = pa_ref[0, :, PA_WI:PA_WI + 128]


def dsa_prep(pa, cos_a, sin_a, b, s):
    tq = min(512, s)
    blk = lambda w: pl.BlockSpec((1, tq, w), lambda bi, i: (bi, i, 0))
    r3 = lambda a: a.reshape(b, s, a.shape[-1])
    shp = lambda w, dt: jax.ShapeDtypeStruct((b, s, w), dt)
    return pl.pallas_call(
        _dsa_prep_kernel,
        grid=(b, s // tq),
        in_specs=[blk(PA_W), blk(LANES), blk(LANES)],
        out_specs=[blk(1024), blk(512), blk(256), blk(128),
                   pl.BlockSpec((1, 256, tq), lambda bi, i: (bi, 0, i)), blk(128)],
        out_shape=[shp(1024, BF16), shp(512, BF16), shp(256, BF16), shp(128, BF16),
                   jax.ShapeDtypeStruct((b, 256, s), BF16), shp(128, F32)],
        compiler_params=_cp(("parallel", "parallel")),
        name="dsa_prep",
    )(r3(pa), r3(cos_a), r3(sin_a))


def _dsa_kernel(q_ref, qi_ref, wi_ref, k_ref, ki_ref, vt_ref, o_ref, keys_ref, hi_ref, lo_ref, *, kc, topk):
    qb = Q_BLOCK
    i = pl.program_id(1)
    n_kc = (i * qb + qb + kc - 1) // kc
    rep = ATT_HEADS // ATT_KV

    qi = qi_ref[0]
    qis = jnp.concatenate([qi[:, h * LANES:(h + 1) * LANES] for h in range(IDX_HEADS)], axis=0)
    wit = wi_ref[0].T
    w_all = jnp.concatenate([wit[h:h + 1, :] for h in range(IDX_HEADS)], axis=1)
    kpos = lax.broadcasted_iota(I32, (kc, qb), 0)
    qpos = lax.broadcasted_iota(I32, (kc, qb), 1) + i * qb

    def score_body(c, carry):
        kic = ki_ref[0, pl.ds(pl.multiple_of(c * kc, kc), kc), :]
        d = lax.dot_general(kic, qis, _NT, preferred_element_type=F32)
        d = jnp.maximum(d, 0.0) * w_all
        acc = d[:, 0:qb]
        for h in range(1, IDX_HEADS):
            acc = acc + d[:, h * qb:(h + 1) * qb]
        acc = jnp.where(acc == 0.0, 0.0, acc)
        bits = pltpu.bitcast(acc, I32)
        key = jnp.where(bits < 0, bits ^ 0x7FFFFFFF, bits)
        key = jnp.where(kpos + c * kc <= qpos, key, INT_MIN)
        keys_ref[c] = key
        hi_ref[c] = jnp.right_shift(key, 16).astype(I16)
        lo_ref[c] = ((key & 0xFFFF) - HALF16).astype(I16)
        return carry

    lax.fori_loop(0, n_kc, score_body, 0)

    pk = 16

    def count16(ref, cand):
        cand16 = jnp.broadcast_to(cand.astype(I16), (kc, qb))

        def body(c, acc):
            sel = jnp.where(ref[c] >= cand16, jnp.ones((), BF16), jnp.zeros((), BF16))
            parts = [sel[j * pk:(j + 1) * pk] for j in range(kc // pk)]
            while len(parts) > 1:
                parts = [parts[j] + parts[j + 1] for j in range(0, len(parts), 2)]
            return acc + parts[0].astype(F32)

        acc = lax.fori_loop(0, n_kc, body, jnp.zeros((pk, qb), F32))
        return jnp.sum(acc, axis=0, keepdims=True)

    def bisect16(ref, target):
        cnt0 = count16(ref, jnp.zeros((1, qb), I32))
        ok0 = cnt0 >= target
        init = (jnp.where(ok0, 0, -HALF16).astype(I32), jnp.where(ok0, cnt0, 0.0), jnp.where(ok0, 0.0, cnt0))

        def body(it, carry):
            ans, cnt_ans, cnt_above = carry
            cand = ans | jnp.left_shift(jnp.int32(1), 14 - it)
            cnt = count16(ref, cand)
            ok = cnt >= target
            return jnp.where(ok, cand, ans), jnp.where(ok, cnt, cnt_ans), jnp.where(ok, cnt_above, cnt)

        return lax.fori_loop(0, 15, body, init)

    kf = jnp.full((1, qb), float(topk), F32)
    p_hi, cnt_hi, cnt_gt = bisect16(hi_ref, kf)
    need_lo = kf - cnt_gt
    p_hi16 = jnp.broadcast_to(p_hi.astype(I16), (kc, qb))

    def bucket_body(c, carry):
        lo_ref[c] = jnp.where(hi_ref[c] == p_hi16, lo_ref[c], jnp.full((), -HALF16, I16))
        return carry

    lax.fori_loop(0, n_kc, bucket_body, 0)
    p_lo, cnt_lo, _ = bisect16(lo_ref, need_lo)
    ans = jnp.left_shift(p_hi, 16) + (p_lo + HALF16)
    cnt_ans = jnp.where(p_lo > -HALF16, cnt_gt + cnt_lo, cnt_hi)

    tie = (ans > INT_MIN) & (cnt_ans > kf)
    any_tie = jnp.max(jnp.where(tie, 1.0, 0.0)) > 0.0

    @pl.when(any_tie)
    def _():
        ansb, ans1b = jnp.broadcast_to(ans, (kc, qb)), jnp.broadcast_to(ans + 1, (kc, qb))
        tieb = jnp.broadcast_to(jnp.where(tie, 1.0, 0.0), (kc, qb)) > 0.0
        lower = (lax.broadcasted_iota(I32, (kc, kc), 0) >= lax.broadcasted_iota(I32, (kc, kc), 1))
        lower = jnp.where(lower, 1.0, 0.0).astype(BF16)

        def gt_body(c, acc):
            return acc + jnp.sum(jnp.where(keys_ref[c] >= ans1b, 1.0, 0.0), axis=0, keepdims=True)

        need = kf - lax.fori_loop(0, n_kc, gt_body, jnp.zeros((1, qb), F32))

        def tie_body(c, run):
            blk = keys_ref[c]
            eq = (blk == ansb) & tieb
            eqf = jnp.where(eq, 1.0, 0.0).astype(BF16)
            pref = jnp.dot(lower, eqf, preferred_element_type=F32) + run
            keys_ref[c] = jnp.where(eq & (pref > need), INT_MIN, blk)
            return pref[kc - 1:kc, :]

        lax.fori_loop(0, n_kc, tie_body, jnp.zeros((1, qb), F32))

    thr = jnp.broadcast_to(jnp.maximum(ans, INT_MIN + 1), (kc, qb))

    q = q_ref[0]
    qg = [jnp.concatenate([q[:, (g * rep + r) * LANES:(g * rep + r + 1) * LANES] for r in range(rep)], axis=0)
          for g in range(ATT_KV)]

    last_chunk = keys_ref.shape[0] - 1

    def att_body(j, carry):
        carry = list(carry)

        def prep(u):
            cu = 2 * j + u
            cc = jnp.minimum(cu, last_chunk)
            bias = jnp.where(keys_ref[cc] >= thr, 0.0, NEG_BIG)
            bias = jnp.where(cu < n_kc, bias, NEG_BIG)
            keys_c = pl.ds(pl.multiple_of(cc * kc, kc), kc)
            return jnp.concatenate([bias] * rep, axis=1), keys_c

        def scores(g, bias, keys_c):
            kg = k_ref[0, keys_c, g * LANES:(g + 1) * LANES]
            return lax.dot_general(kg, qg[g], _NT, preferred_element_type=F32) + bias

        def softmax(g, st):
            m_old = carry[2 * g]
            m_new = jnp.maximum(m_old, jnp.max(st, axis=0, keepdims=True))
            carry[2 * g] = m_new
            return jnp.exp2(m_old - m_new), jnp.exp2(st - m_new).astype(BF16)

        def accumulate(g, alpha, pt, keys_c):
            pv = jnp.dot(vt_ref[0, g * LANES:(g + 1) * LANES, keys_c], pt, preferred_element_type=F32)
            carry[2 * g + 1] = carry[2 * g + 1] * alpha + pv

        bias_a, keys_a = prep(0)
        bias_b, keys_b = prep(1)
        st0a = scores(0, bias_a, keys_a)
        st1a = scores(1, bias_a, keys_a)
        al0a, p0a = softmax(0, st0a)
        st0b = scores(0, bias_b, keys_b)
        accumulate(0, al0a, p0a, keys_a)
        al1a, p1a = softmax(1, st1a)
        st1b = scores(1, bias_b, keys_b)
        accumulate(1, al1a, p1a, keys_a)
        al0b, p0b = softmax(0, st0b)
        accumulate(0, al0b, p0b, keys_b)
        al1b, p1b = softmax(1, st1b)
        accumulate(1, al1b, p1b, keys_b)
        return tuple(carry)

    init = []
    for g in range(ATT_KV):
        init += [jnp.full((1, rep * qb), NEG_BIG, F32), jnp.zeros((LANES, rep * qb), F32)]
    res = lax.fori_loop(0, (n_kc + 1) // 2, att_body, tuple(init))

    outs = []
    for g in range(ATT_KV):
        acc = res[2 * g + 1]
        ot = acc[:ATT_DH] / acc[ATT_DH:ATT_DH + 1]
        outs += [ot[:, r * qb:(r + 1) * qb] for r in range(rep)]
    o_ref[0] = jnp.concatenate(outs, axis=0).T.astype(o_ref.dtype)


def dsa_attention(q, qi, wi, k, ki, vt, b, s):
    kc = min(DSA_KC, s)
    topk = min(INDEX_TOPK, s // 4)
    qblk = lambda w: pl.BlockSpec((1, Q_BLOCK, w), lambda bi, i: (bi, i, 0))
    full = lambda w: pl.BlockSpec((1, s, w), lambda bi, i: (bi, 0, 0))
    out = pl.pallas_call(
        functools.partial(_dsa_kernel, kc=kc, topk=topk),
        grid=(b, s // Q_BLOCK),
        in_specs=[qblk(1024), qblk(512), qblk(128), full(256), full(128),
                  pl.BlockSpec((1, 256, s), lambda bi, i: (bi, 0, 0))],
        out_specs=qblk(BRANCH),
        out_shape=jax.ShapeDtypeStruct((b, s, BRANCH), BF16),
        scratch_shapes=[pltpu.VMEM((s // kc, kc, Q_BLOCK), I32), pltpu.VMEM((s // kc, kc, Q_BLOCK), I16),
                        pltpu.VMEM((s // kc, kc, Q_BLOCK), I16)],
        compiler_params=_cp(("parallel", "arbitrary")),
        name="dsa_attention",
    )(q, qi, wi, k, ki, vt)
    return out.reshape(b * s, BRANCH)


def _rwkv_kernel(pb_ref, mu_ref, vec_ref, lora_ref, bones_ref, o_ref,
                 carry_ref, state_ref, r_s, k_s, v_s, kk_s, b_s, ld_s, y_s, *, exact):
    tt = pb_ref.shape[1]
    c_len = RW_C
    t_idx = pl.program_id(1)

    @pl.when(t_idx == 0)
    def _():
        carry_ref[...] = jnp.zeros_like(carry_ref)
        state_ref[...] = jnp.zeros_like(state_ref)

    p = pb_ref[0]
    prev = pltpu.roll(p, 1, axis=0)
    first = lax.broadcasted_iota(I32, p.shape, 0) == 0
    prev = jnp.where(first, jnp.broadcast_to(carry_ref[7:8, :], p.shape), prev)
    carry_ref[...] = p[tt - 8:, :]
    p = p + (prev - p) * mu_ref[...]

    w0, a0, k_k, k_a = vec_ref[0:1, :], vec_ref[1:2, :], vec_ref[2:3, :], vec_ref[3:4, :]
    r_k, ln_g, ln_b = vec_ref[4:5, :], vec_ref[5:6, :], vec_ref[6:7, :]
    bones = bones_ref[...]

    r = p[:, 0:512]
    k = p[:, 512:1024]
    v = p[:, 1024:1536]
    lo = p[:, 1536:1792]
    wl = _dot(jnp.tanh(lo), lora_ref[0])
    al = _dot(lo, lora_ref[1])
    g = _dot(jax.nn.sigmoid(lo), lora_ref[2])
    z = -(w0 + wl)
    w = -(jnp.maximum(z, 0.0) + jnp.log(1.0 + jnp.exp(-jnp.abs(z)))) - 0.5
    a = jax.nn.sigmoid(a0 + al)
    kk = k * k_k
    ssq = _dot_split(kk * kk, bones)
    kk = kk / jnp.maximum(jnp.sqrt(ssq), 1e-12)
    k = k * (1.0 + (a - 1.0) * k_a)
    r_s[...] = r
    k_s[...] = k
    v_s[...] = v
    kk_s[...] = kk
    b_s[...] = kk * a
    ld_s[...] = -jnp.exp(w)

    ii = lax.broadcasted_iota(I32, (c_len, c_len), 0)
    jj = lax.broadcasted_iota(I32, (c_len, c_len), 1)
    strict, incl = ii > jj, ii >= jj
    eye = jnp.where(ii == jj, 1.0, 0.0)
    tri = jnp.where(incl, 1.0, 0.0).astype(BF16)
    blk_masks = {1 << sh: (ii >> sh) == (jj >> sh) for sh in (3, 4, 5, 6)}

    dot = functools.partial(_dot, exact=exact)
    off_masks = [blk_masks[sz] & jnp.logical_not(blk_masks[sz // 2]) for sz in (16, 32, 64)]

    def chunk_body(ci, carry):
        probs = []
        for u in range(RW_PAR):
            rows = pl.ds(pl.multiple_of((ci * RW_PAR + u) * c_len, c_len), c_len)
            ld = ld_s[rows, :]
            cum = _dot_split(ld, tri, left=True)
            cum_end = cum[c_len - 1:c_len, :]
            e_inv, e_end = jnp.exp(-cum), jnp.exp(cum_end - cum)
            kc_, bc_ = k_s[rows, :], b_s[rows, :]
            kkt = kk_s[rows, :] * jnp.exp(cum - ld)
            rt = r_s[rows, :] * jnp.exp(cum)
            kd, bd = kc_ * e_inv, bc_ * e_inv
            kh, bh = kc_ * e_end, bc_ * e_end
            vc = v_s[rows, :]
            e_last = jnp.exp(cum_end)
            for h in range(RW_H):
                sl = slice(h * RW_N, (h + 1) * RW_N)
                probs.append(dict(rows=rows, sl=sl, h=h, kkt=kkt[:, sl], rt=rt[:, sl], kd=kd[:, sl], bd=bd[:, sl],
                                  kh=kh[:, sl], bh=bh[:, sl], v=vc[:, sl], e_last=e_last[:, sl]))
        for p in probs:
            big = dot(jnp.concatenate([p["kkt"], p["rt"]], axis=0),
                      jnp.concatenate([p["bd"], p["kd"]], axis=0), _NT)
            p["m_b"] = jnp.where(strict, big[:c_len, :c_len], 0.0)
            p["m_k"] = jnp.where(strict, big[:c_len, c_len:], 0.0)
            p["p_b"] = jnp.where(incl, big[c_len:, :c_len], 0.0)
            p["p_k"] = jnp.where(incl, big[c_len:, c_len:], 0.0)
            p["n8"] = jnp.where(blk_masks[8], p["m_b"], 0.0)
        for p in probs:
            p["n2"] = dot(p["n8"], p["n8"])
            p["mkv"] = dot(p["m_k"], p["v"])
        for p in probs:
            p["n4"] = dot(p["n2"], p["n2"])
            p["t"] = dot(eye - p["n8"], eye + p["n2"])
        for p in probs:
            p["t"] = dot(p["t"], eye + p["n4"])
        for om in off_masks:
            for p in probs:
                p["a"] = dot(p["t"], jnp.where(om, p["m_b"], 0.0))
            for p in probs:
                p["t"] = p["t"] - dot(p["a"], p["t"])
        for p in probs:
            p["wu"] = dot(p["t"], jnp.concatenate([p["kkt"], p["mkv"]], axis=1))
            p["pkv"] = dot(p["p_k"], p["v"])
        for p in probs:
            pwu = dot(p["p_b"], p["wu"])
            p["y0"] = p["pkv"] - pwu[:, RW_N:]
            p["gm"] = p["rt"] - pwu[:, :RW_N]
            p["bw"] = dot(p["bh"], p["wu"][:, :RW_N], _TN)
            p["psi_t"] = dot(jnp.concatenate([p["v"], p["wu"][:, RW_N:]], axis=0),
                             jnp.concatenate([p["kh"], -p["bh"]], axis=0), _TN)
        for p in probs:
            st = state_ref[p["h"]]
            y_s[p["rows"], p["sl"]] = p["y0"] + _dot(p["gm"], st, _NT, exact=True)
            state_ref[p["h"]] = st * p["e_last"] - _dot(st, p["bw"], _NT, exact=True) + p["psi_t"]
        return carry

    lax.fori_loop(0, tt // (c_len * RW_PAR), chunk_body, 0)

    y = y_s[...]
    r, k, v = r_s[...], k_s[...], v_s[...]
    mean = _dot_split(y, bones) * (1.0 / RW_N)
    yc = y - mean
    var = _dot_split(yc * yc, bones) * (1.0 / RW_N)
    y = yc * lax.rsqrt(var + RW_GN_EPS) * ln_g + ln_b
    y = y + _dot_split(r * k * r_k, bones) * v
    o_ref[0] = (y * g).astype(o_ref.dtype)


def rwkv_mix(pb, mu, vecs, lora, b, s, exact=False):
    tt = min(RW_TT, s)
    bones = (jnp.arange(BRANCH)[:, None] // RW_N == jnp.arange(BRANCH)[None, :] // RW_N).astype(BF16)
    const = lambda shape: pl.BlockSpec(shape, lambda bi, ti: (0,) * len(shape))
    sq = lambda: pltpu.VMEM((tt, BRANCH), F32)
    out = pl.pallas_call(
        functools.partial(_rwkv_kernel, exact=exact),
        grid=(b, s // tt),
        in_specs=[pl.BlockSpec((1, tt, PB_W), lambda bi, ti: (bi, ti, 0)),
                  const((1, PB_W)), const((8, BRANCH)), const((3, 256, BRANCH)), const((BRANCH, BRANCH))],
        out_specs=pl.BlockSpec((1, tt, BRANCH), lambda bi, ti: (bi, ti, 0)),
        out_shape=jax.ShapeDtypeStruct((b, s, BRANCH), BF16),
        scratch_shapes=[pltpu.VMEM((8, PB_W), F32), pltpu.VMEM((RW_H, RW_N, RW_N), F32),
                        sq(), sq(), sq(), sq(), sq(), sq(), sq()],
        compiler_params=_cp(("parallel", "arbitrary")),
        name="rwkv_mix",
    )(pb.reshape(b, s, PB_W), mu, vecs, lora, bones)
    return out.reshape(b * s, BRANCH)


def _pool_kernel(pc_ref, w_ref, sc_ref, o_ref, carry_ref):
    ts = pc_ref.shape[1]
    t_idx = pl.program_id(1)

    @pl.when(t_idx == 0)
    def _():
        carry_ref[...] = jnp.zeros_like(carry_ref)

    x = pc_ref[0]
    ext = jnp.concatenate([carry_ref[...], x], axis=0)
    carry_ref[...] = x[ts - 16:, :]
    n = ts + 16
    s2 = ext + pltpu.roll(ext, 1, axis=0)
    s4 = s2[:, 128:] + pltpu.roll(s2[:, 128:], 2, axis=0)
    s8 = s4[:, 128:] + pltpu.roll(s4[:, 128:], 4, axis=0)
    s16 = s8[:, 128:] + pltpu.roll(s8[:, 128:], 8, axis=0)
    sums = (s2[16:, 0:128], s4[16:, 0:128], s8[16:, 0:128], s16[16:, 0:128])
    step = (lax.broadcasted_iota(I32, (ts, POOL_GD), 0) + t_idx * ts + 1).astype(F32)
    outs = []
    for gi, win in enumerate(POOL_WINDOWS):
        mean = sums[gi] / jnp.minimum(step, float(win))
        pooled = mean - x[:, gi * POOL_GD:(gi + 1) * POOL_GD]
        outs.append(jnp.dot(pooled.astype(BF16), w_ref[gi], preferred_element_type=F32))
    del n
    o_ref[0] = (jnp.concatenate(outs, axis=1) * sc_ref[...]).astype(o_ref.dtype)


def pool_mix(pc, pool_w, pool_scale, b, s):
    ts = min(POOL_TS, s)
    out = pl.pallas_call(
        _pool_kernel,
        grid=(b, s // ts),
        in_specs=[pl.BlockSpec((1, ts, BRANCH), lambda bi, ti: (bi, ti, 0)),
                  pl.BlockSpec((4, POOL_GD, POOL_GD), lambda bi, ti: (0, 0, 0)),
                  pl.BlockSpec((1, BRANCH), lambda bi, ti: (0, 0))],
        out_specs=pl.BlockSpec((1, ts, BRANCH), lambda bi, ti: (bi, ti, 0)),
        out_shape=jax.ShapeDtypeStruct((b, s, BRANCH), BF16),
        scratch_shapes=[pltpu.VMEM((16, BRANCH), F32)],
        compiler_params=_cp(("parallel", "arbitrary")),
        name="pool_mix",
    )(pc.reshape(b, s, BRANCH), pool_w, pool_scale)
    return out.reshape(b * s, BRANCH)


def _ret_kernel(pd_ref, cos_ref, sin_ref, gn_ref, bones_ref, o_ref, state_ref, o_s):
    c_len = pd_ref.shape[1]
    t_idx = pl.program_id(1)

    @pl.when(t_idx == 0)
    def _():
        state_ref[...] = jnp.zeros_like(state_ref)

    cos, sin = cos_ref[0], sin_ref[0]
    q = _rope_apply(pd_ref[0, :, 0:512], cos, sin, RET_DK)
    k = _rope_apply(pd_ref[0, :, 512:1024], cos, sin, RET_DK) * (RET_DK ** -0.5)
    v = pd_ref[0, :, 1024:1536]
    gate = pd_ref[0, :, 1536:2048]

    ii = lax.broadcasted_iota(I32, (c_len, c_len), 0)
    jj = lax.broadcasted_iota(I32, (c_len, c_len), 1)
    diff = (ii - jj).astype(F32)
    pos = lax.broadcasted_iota(I32, (c_len, 1), 0).astype(F32)
    for h in range(RET_H):
        log_gamma = math.log(1.0 - 2.0 ** (-5.0 - h))
        intra = jnp.where(diff >= 0, jnp.exp(jnp.maximum(diff, 0.0) * log_gamma), 0.0)
        q_decay = jnp.exp((pos + 1.0) * log_gamma)
        k_decay = jnp.exp((c_len - 1.0 - pos) * log_gamma)
        c_decay = math.exp(c_len * log_gamma)
        sl = slice(h * 64, (h + 1) * 64)
        qh, kh, vh = q[:, sl], k[:, sl], v[:, sl]
        st = state_ref[h]
        sc = _dot(qh, kh, _NT) * intra
        o = _dot(sc, vh) + _dot(qh, st) * q_decay
        state_ref[h] = st * c_decay + _dot(kh * k_decay, vh, _TN)
        o_s[:, sl] = o

    o = o_s[...]
    bones = bones_ref[...]
    mean = _dot_split(o, bones) * (1.0 / RET_DV)
    oc = o - mean
    var = _dot_split(oc * oc, bones) * (1.0 / RET_DV)
    o = oc * lax.rsqrt(var + RET_GN_EPS) * gn_ref[...]
    o_ref[0] = (jax.nn.silu(gate) * o).astype(o_ref.dtype)


def ret_mix(pd, cos_r, sin_r, gn_g, b, s):
    c_len = min(RET_C, s)
    bones = (jnp.arange(BRANCH)[:, None] // RET_DV == jnp.arange(BRANCH)[None, :] // RET_DV).astype(BF16)
    out = pl.pallas_call(
        _ret_kernel,
        grid=(b, s // c_len),
        in_specs=[pl.BlockSpec((1, c_len, PD_W), lambda bi, ti: (bi, ti, 0)),
                  pl.BlockSpec((1, c_len, LANES), lambda bi, ti: (bi, ti, 0)),
                  pl.BlockSpec((1, c_len, LANES), lambda bi, ti: (bi, ti, 0)),
                  pl.BlockSpec((1, BRANCH), lambda bi, ti: (0, 0)),
                  pl.BlockSpec((BRANCH, BRANCH), lambda bi, ti: (0, 0))],
        out_specs=pl.BlockSpec((1, c_len, BRANCH), lambda bi, ti: (bi, ti, 0)),
        out_shape=jax.ShapeDtypeStruct((b, s, BRANCH), BF16),
        scratch_shapes=[pltpu.VMEM((RET_H, 64, RET_DV), F32), pltpu.VMEM((c_len, BRANCH), F32)],
        compiler_params=_cp(("parallel", "arbitrary")),
        name="ret_mix",
    )(pd.reshape(b, s, PD_W), cos_r.reshape(b, s, LANES), sin_r.reshape(b, s, LANES), gn_g, bones)
    return out.reshape(b * s, BRANCH)


def _merge_kernel(x_ref, pg_ref, oa_ref, ob_ref, oc_ref, od_ref, gb_ref, wb_ref, wo_ref, o_ref):
    merged = None
    for i, br in enumerate((oa_ref, ob_ref, oc_ref, od_ref)):
        gate = jax.nn.sigmoid(pg_ref[:, i * D_MODEL:(i + 1) * D_MODEL].astype(F32) + gb_ref[i:i + 1, :])
        term = gate * jnp.dot(br[...], wb_ref[i], preferred_element_type=F32)
        merged = term if merged is None else merged + term
    o_ref[...] = x_ref[...] + jnp.dot(merged.astype(BF16), wo_ref[...], preferred_element_type=F32)


def merge_out(x, pg, o_a, o_b, o_c, o_d, gate_b, w_branch, w_out):
    t = x.shape[0]
    tm = min(MERGE_TM, t)
    row = lambda w: pl.BlockSpec((tm, w), lambda i: (i, 0))
    return pl.pallas_call(
        _merge_kernel,
        grid=(t // tm,),
        in_specs=[row(D_MODEL), row(4 * D_MODEL), row(BRANCH), row(BRANCH), row(BRANCH), row(BRANCH),
                  pl.BlockSpec((4, D_MODEL), lambda i: (0, 0)),
                  pl.BlockSpec((4, BRANCH, D_MODEL), lambda i: (0, 0, 0)),
                  pl.BlockSpec((D_MODEL, D_MODEL), lambda i: (0, 0))],
        out_specs=row(D_MODEL),
        out_shape=jax.ShapeDtypeStruct((t, D_MODEL), F32),
        compiler_params=_cp(("parallel",)),
        name="merge_out",
    )(x, pg, o_a, o_b, o_c, o_d, gate_b, w_branch, w_out)


def _mlp_kernel(x_ref, g_ref, up_ref, down_ref, fg_ref, o_ref, h_ref, acc_ref, *, final_norm):
    j = pl.program_id(1)

    @pl.when(j == 0)
    def _():
        x = x_ref[...]
        ms = jnp.mean(x * x, axis=-1, keepdims=True)
        h_ref[...] = (x * lax.rsqrt(ms + NORM_EPS) * g_ref[...]).astype(BF16)
        acc_ref[...] = x

    u = jnp.dot(h_ref[...], up_ref[...], preferred_element_type=F32)
    u = jnp.square(jnp.maximum(u, 0.0))
    acc_ref[...] += jnp.dot(u.astype(BF16), down_ref[...], preferred_element_type=F32)

    @pl.when(j == pl.num_programs(1) - 1)
    def _():
        y = acc_ref[...]
        if final_norm:
            ms = jnp.mean(y * y, axis=-1, keepdims=True)
            y = y * lax.rsqrt(ms + NORM_EPS) * fg_ref[...]
        o_ref[...] = y


def mlp_block(x, g, up, down, final_g, final_norm):
    t = x.shape[0]
    tm = min(MLP_TM, t)
    return pl.pallas_call(
        functools.partial(_mlp_kernel, final_norm=final_norm),
        grid=(t // tm, D_FF // MLP_FF),
        in_specs=[pl.BlockSpec((tm, D_MODEL), lambda i, j: (i, 0)),
                  pl.BlockSpec((1, D_MODEL), lambda i, j: (0, 0)),
                  pl.BlockSpec((D_MODEL, MLP_FF), lambda i, j: (0, j)),
                  pl.BlockSpec((MLP_FF, D_MODEL), lambda i, j: (j, 0)),
                  pl.BlockSpec((1, D_MODEL), lambda i, j: (0, 0))],
        out_specs=pl.BlockSpec((tm, D_MODEL), lambda i, j: (i, 0)),
        out_shape=jax.ShapeDtypeStruct((t, D_MODEL), F32),
        scratch_shapes=[pltpu.VMEM((tm, D_MODEL), BF16), pltpu.VMEM((tm, D_MODEL), F32)],
        compiler_params=_cp(("parallel", "arbitrary")),
        name="mlp_block",
    )(x, g, up, down, final_g)


def _partner_cols(w, heads, dh, rot):
    lead = w.shape[:-1]
    w3 = w.reshape(*lead, heads, dh)
    half = rot // 2
    part = jnp.concatenate([-w3[..., half:rot], w3[..., :half], jnp.zeros_like(w3[..., rot:])], axis=-1)
    return w3, part


def _interleave(w, heads, dh, rot):
    w3, part = _partner_cols(w, heads, dh, rot)
    return jnp.concatenate([w3, part], axis=-1).reshape(*w.shape[:-1], heads * 2 * dh)


def _prep_weights(w_in, rwkv_mu):
    n_a = 512 + 128 + 128 + 256 + 64 + 4
    n_b = 1696
    off_b, off_c, off_d = n_a, n_a + n_b, n_a + n_b + 512
    off_g = off_d + 1536
    wa = w_in[..., :n_a]
    q, k, v = wa[..., 0:512], wa[..., 512:640], wa[..., 640:768]
    qi, ki, wi = wa[..., 768:1024], wa[..., 1024:1088], wa[..., 1088:1092]
    pad = lambda w, n: jnp.pad(w, [(0, 0)] * (w.ndim - 1) + [(0, n - w.shape[-1])])
    w_a = jnp.concatenate([_interleave(q, ATT_HEADS, ATT_DH, ROT_DIM), _interleave(qi, IDX_HEADS, IDX_DH, ROT_DIM),
                           _interleave(k, ATT_KV, ATT_DH, ROT_DIM), _interleave(ki, 1, IDX_DH, ROT_DIM),
                           v, pad(wi, 128)], axis=-1)
    w_a = pad(w_a, PA_W)
    wb = w_in[..., off_b:off_c]
    perm = lambda z: jnp.concatenate([z[..., 0:512], z[..., 544:1056], z[..., 1056:1568],
                                      z[..., 512:544], z[..., 1568:1600], z[..., 1600:1696]], axis=-1)
    w_b = pad(perm(wb), PB_W)
    mu = pad(perm(rwkv_mu), PB_W)
    w_c = w_in[..., off_c:off_d]
    wd = w_in[..., off_d:off_g]
    rq, rk, rv, rg = wd[..., 0:256], wd[..., 256:512], wd[..., 512:1024], wd[..., 1024:1536]
    w_d = jnp.concatenate([_interleave(rq, RET_H, RET_DK, RET_DK), _interleave(rk, RET_H, RET_DK, RET_DK), rv, rg],
                          axis=-1)
    w_g = w_in[..., off_g:]
    cast = lambda w: w.astype(BF16)
    return cast(w_a), cast(w_b), cast(w_c), cast(w_d), cast(w_g), mu


def kernel(x, positions, attn_norm_g, w_in, rwkv_mu, rwkv_w0, rwkv_w2, rwkv_a0, rwkv_a2, rwkv_g2,
           rwkv_k_k, rwkv_k_a, rwkv_r_k, rwkv_ln_g, rwkv_ln_b, pool_w, pool_scale, ret_gn_g,
           gate_b, w_branch, w_out, mlp_norm_g, mlp_up, mlp_down, final_norm_g):
    b, s, d = x.shape
    depth = w_in.shape[0]
    t = b * s
    w_a, w_b, w_c, w_d, w_g, mu = _prep_weights(w_in, rwkv_mu)
    zeros = jnp.zeros_like(rwkv_w0)
    vecs = jnp.stack([rwkv_w0, rwkv_a0, rwkv_k_k, rwkv_k_a, rwkv_r_k, rwkv_ln_g, rwkv_ln_b, zeros], axis=1)
    lora = jnp.zeros((depth, 3, 256, BRANCH), F32)
    lora = lora.at[:, 0, 0:RW_DL].set(rwkv_w2)
    lora = lora.at[:, 1, RW_DL:RW_DL + RW_AL].set(rwkv_a2)
    lora = lora.at[:, 2, RW_DL + RW_AL:RW_DL + RW_AL + RW_GL].set(rwkv_g2)
    lora = lora.astype(BF16)
    pool_w16, w_branch16, w_out16 = pool_w.astype(BF16), w_branch.astype(BF16), w_out.astype(BF16)
    up16, down16 = mlp_up.astype(BF16), mlp_down.astype(BF16)

    cos_a, sin_a, cos_r, sin_r = rope_tables(positions)
    xf = x.reshape(t, d)
    for l in range(depth):
        g = attn_norm_g[l][None]
        pa = norm_matmul(xf, g, w_a[l], F32)
        pb = norm_matmul(xf, g, w_b[l], F32)
        pc = norm_matmul(xf, g, w_c[l], F32)
        pd = norm_matmul(xf, g, w_d[l], F32)
        pg = norm_matmul(xf, g, w_g[l], BF16)
        q, qi, k, ki, vt, wi = dsa_prep(pa, cos_a, sin_a, b, s)
        o_a = dsa_attention(q, qi, wi, k, ki, vt, b, s)
        o_b = rwkv_mix(pb, mu[l][None], vecs[l], lora[l], b, s)
        o_c = pool_mix(pc, pool_w16[l], pool_scale[l][None], b, s)
        o_d = ret_mix(pd, cos_r, sin_r, ret_gn_g[l][None], b, s)
        xf = merge_out(xf, pg, o_a, o_b, o_c, o_d, gate_b[l], w_branch16[l], w_out16[l])
        xf = mlp_block(xf, mlp_norm_g[l][None], up16[l], down16[l], final_norm_g[None], l == depth - 1)
    return xf.reshape(b, s, d)
```

```python
import functools
import math

import jax
import jax.numpy as jnp
from jax import lax
from jax.experimental import pallas as pl
from jax.experimental.pallas import tpu as pltpu

F32 = jnp.float32
BF16 = jnp.bfloat16
I32 = jnp.int32

D_MODEL = 1024
BRANCH = 512
ATT_HEADS, ATT_KV, ATT_DH, ROT_DIM = 8, 2, 64, 16
ROPE_THETA = 500000.0
IDX_HEADS, IDX_DH, INDEX_TOPK, Q_BLOCK = 4, 64, 256, 128
RW_N, RW_H = 64, 8
RW_DL, RW_AL, RW_GL = 32, 32, 96
RW_GN_EPS = 64e-5
POOL_WINDOWS = (2, 4, 8, 16)
POOL_GD = 128
RET_H, RET_DK, RET_DV = 8, 32, 64
RET_THETA = 10000.0
RET_GN_EPS = 1e-6
D_FF = 4096
NORM_EPS = 1e-5

LANES = 128
VMEM_LIMIT = 56 * 1024 * 1024

PROJ_TM = 1024
DSA_KC = 512
RW_C = 64
RW_TT = 512
RW_PAR = 2
POOL_TS = 512
RET_C = 256
MERGE_TM = 512
MLP_TM = 512
MLP_FF = 1024

INT_MIN = -2 ** 31
SUBLANES = 8
GROUP_KEYS = 32 * SUBLANES
LOG2E = 1.4426950408889634
NEG_BIG = -1e30

PA_W = 2304
PA_Q, PA_QI, PA_K, PA_KI, PA_V, PA_WI = 0, 1024, 1536, 1792, 1920, 2048
PB_W = 1792
PD_W = 2048


def _cp(sem, vmem=VMEM_LIMIT):
    return pltpu.CompilerParams(dimension_semantics=sem, vmem_limit_bytes=vmem)


def _dot(a, b, dims=(((1,), (0,)), ((), ())), exact=False):
    if exact:
        return lax.dot_general(a.astype(F32), b.astype(F32), dims, precision=lax.Precision.HIGHEST,
                               preferred_element_type=F32)
    return lax.dot_general(a.astype(BF16), b.astype(BF16), dims, preferred_element_type=F32)


_NN = (((1,), (0,)), ((), ()))
_NT = (((1,), (1,)), ((), ()))
_TN = (((0,), (0,)), ((), ()))


def _dot_split(x, ones_rhs, dims=_NN, left=False):
    hi = x.astype(BF16)
    lo = (x - hi.astype(F32)).astype(BF16)
    if left:
        return (lax.dot_general(ones_rhs, hi, dims, preferred_element_type=F32)
                + lax.dot_general(ones_rhs, lo, dims, preferred_element_type=F32))
    return (lax.dot_general(hi, ones_rhs, dims, preferred_element_type=F32)
            + lax.dot_general(lo, ones_rhs, dims, preferred_element_type=F32))


def _norm_matmul_kernel(x_ref, g_ref, w_ref, o_ref, h_ref):
    @pl.when(pl.program_id(1) == 0)
    def _():
        x = x_ref[...]
        ms = jnp.mean(x * x, axis=-1, keepdims=True)
        h_ref[...] = (x * lax.rsqrt(ms + NORM_EPS) * g_ref[...]).astype(BF16)

    o_ref[...] = jnp.dot(h_ref[...], w_ref[...], preferred_element_type=F32).astype(o_ref.dtype)


def _pick_tn(n):
    for tn in (1024, 896, 768, 640, 512, 384, 256, 128):
        if n % tn == 0:
            return tn
    raise ValueError(n)


def norm_matmul(x, g, w, out_dtype):
    t, d = x.shape
    n = w.shape[1]
    tm, tn = min(PROJ_TM, t), _pick_tn(n)
    return pl.pallas_call(
        _norm_matmul_kernel,
        grid=(t // tm, n // tn),
        in_specs=[pl.BlockSpec((tm, d), lambda i, j: (i, 0)),
                  pl.BlockSpec((1, d), lambda i, j: (0, 0)),
                  pl.BlockSpec((d, tn), lambda i, j: (0, j))],
        out_specs=pl.BlockSpec((tm, tn), lambda i, j: (i, j)),
        out_shape=jax.ShapeDtypeStruct((t, n), out_dtype),
        scratch_shapes=[pltpu.VMEM((tm, d), BF16)],
        compiler_params=_cp(("parallel", "arbitrary")),
        name="norm_matmul",
    )(x, g, w)


def _rope_table_kernel(pos_ref, fa_ref, fr_ref, ma_ref, mr_ref, ca_ref, sa_ref, cr_ref, sr_ref):
    pos = pos_ref[...].astype(F32)
    ang_a = pos * fa_ref[...]
    ca_ref[...] = jnp.cos(ang_a) * ma_ref[...]
    sa_ref[...] = jnp.sin(ang_a) * ma_ref[...]
    ang_r = pos * fr_ref[...]
    cr_ref[...] = jnp.cos(ang_r) * mr_ref[...]
    sr_ref[...] = jnp.sin(ang_r) * mr_ref[...]


def rope_tables(positions):
    t = positions.size
    pos = positions.reshape(t, 1)
    inv_a = ROPE_THETA ** (-jnp.arange(0, ROT_DIM, 2, dtype=F32) / ROT_DIM)
    fa = jnp.concatenate([inv_a, inv_a, jnp.zeros((LANES - ROT_DIM,), F32)])[None]
    ma = jnp.concatenate([jnp.ones((ATT_DH,), F32), jnp.zeros((LANES - ATT_DH,), F32)])[None]
    inv_r = 1.0 / (RET_THETA ** jnp.linspace(0.0, 1.0, RET_DK // 2, dtype=F32))
    fr64 = jnp.concatenate([inv_r, inv_r, jnp.zeros((RET_DK,), F32)])
    fr = jnp.concatenate([fr64, fr64])[None]
    mr64 = jnp.concatenate([jnp.ones((RET_DK,), F32), jnp.zeros((RET_DK,), F32)])
    mr = jnp.concatenate([mr64, mr64])[None]
    ts = min(1024, t)
    row = pl.BlockSpec((1, LANES), lambda i: (0, 0))
    tab = pl.BlockSpec((ts, LANES), lambda i: (i, 0))
    return pl.pallas_call(
        _rope_table_kernel,
        grid=(t // ts,),
        in_specs=[pl.BlockSpec((ts, 1), lambda i: (i, 0)), row, row, row, row],
        out_specs=[tab, tab, tab, tab],
        out_shape=[jax.ShapeDtypeStruct((t, LANES), F32)] * 4,
        compiler_params=_cp(("parallel",)),
        name="rope_tables",
    )(pos, fa, fr, ma, mr)


def _rope_apply(p, cos, sin, half_block):
    w = p.shape[-1]
    reps = w // cos.shape[-1]
    c = jnp.concatenate([cos] * reps, axis=-1) if reps > 1 else cos
    s = jnp.concatenate([sin] * reps, axis=-1) if reps > 1 else sin
    partner = pltpu.roll(p, w - half_block, axis=1)
    return p * c + partner * s


def _dsa_prep_kernel(pa_ref, cos_ref, sin_ref, q_ref, qi_ref, k_ref, ki_ref, vt_ref, wi_ref):
    cos, sin = cos_ref[0], sin_ref[0]
    q_scale = ATT_DH ** -0.5 * LOG2E
    q_ref[0] = (_rope_apply(pa_ref[0, :, PA_Q:PA_Q + 1024], cos, sin, ATT_DH) * q_scale).astype(BF16)
    qi_ref[0] = _rope_apply(pa_ref[0, :, PA_QI:PA_QI + 512], cos, sin, ATT_DH).astype(BF16)
    k_ref[0] = _rope_apply(pa_ref[0, :, PA_K:PA_K + 256], cos, sin, ATT_DH).astype(BF16)
    ki_ref[0] = _rope_apply(pa_ref[0, :, PA_KI:PA_KI + 128], cos, sin, ATT_DH).astype(BF16)
    vt = pa_ref[0, :, PA_V:PA_V + 128].T
    ones = jnp.ones((ATT_DH, vt.shape[1]), F32)
    vt_ref[0] = jnp.concatenate([vt[:ATT_DH], ones, vt[ATT_DH:], ones], axis=0).astype(BF16)
    wi_ref[0] = pa_ref[0, :, PA_WI:PA_WI + 128]


def dsa_prep(pa, cos_a, sin_a, b, s):
    tq = min(512, s)
    blk = lambda w: pl.BlockSpec((1, tq, w), lambda bi, i: (bi, i, 0))
    r3 = lambda a: a.reshape(b, s, a.shape[-1])
    shp = lambda w, dt: jax.ShapeDtypeStruct((b, s, w), dt)
    return pl.pallas_call(
        _dsa_prep_kernel,
        grid=(b, s // tq),
        in_specs=[blk(PA_W), blk(LANES), blk(LANES)],
        out_specs=[blk(1024), blk(512), blk(256), blk(128),
                   pl.BlockSpec((1, 256, tq), lambda bi, i: (bi, 0, i)), blk(128)],
        out_shape=[shp(1024, BF16), shp(512, BF16), shp(256, BF16), shp(128, BF16),
                   jax.ShapeDtypeStruct((b, 256, s), BF16), shp(128, F32)],
        compiler_params=_cp(("parallel", "parallel")),
        name="dsa_prep",
    )(r3(pa), r3(cos_a), r3(sin_a))


def _bit_transpose32(words):
    a = list(words)
    for sh, mask in ((16, 0x0000FFFF), (8, 0x00FF00FF), (4, 0x0F0F0F0F), (2, 0x33333333), (1, 0x55555555)):
        for k in range(32):
            if k & sh:
                continue
            t = (jnp.right_shift(a[k], sh) ^ a[k + sh]) & mask
            a[k + sh] = a[k + sh] ^ t
            a[k] = a[k] ^ jnp.left_shift(t, sh)
    return a


def _dsa_kernel(q_ref, qi_ref, wi_ref, k_ref, ki_ref, vt_ref, o_ref, keys_ref, planes_ref, live_ref, *, kc, topk):
    qb = Q_BLOCK
    i = pl.program_id(1)
    n_kc = (i * qb + qb + kc - 1) // kc
    rep = ATT_HEADS // ATT_KV

    qi = qi_ref[0]
    qis = jnp.concatenate([qi[:, h * LANES:(h + 1) * LANES] for h in range(IDX_HEADS)], axis=0)
    wit = wi_ref[0].T
    w_all = jnp.concatenate([wit[h:h + 1, :] for h in range(IDX_HEADS)], axis=1)
    kpos = lax.broadcasted_iota(I32, (kc, qb), 0)
    qpos = lax.broadcasted_iota(I32, (kc, qb), 1) + i * qb

    def score_body(c, carry):
        kic = ki_ref[0, pl.ds(pl.multiple_of(c * kc, kc), kc), :]
        d = lax.dot_general(kic, qis, _NT, preferred_element_type=F32)
        d = jnp.maximum(d, 0.0) * w_all
        acc = d[:, 0:qb]
        for h in range(1, IDX_HEADS):
            acc = acc + d[:, h * qb:(h + 1) * qb]
        acc = jnp.where(acc == 0.0, 0.0, acc)
        bits = pltpu.bitcast(acc, I32)
        key = jnp.where(bits < 0, bits ^ 0x7FFFFFFF, bits)
        key = jnp.where(kpos + c * kc <= qpos, key, INT_MIN)
        keys_ref[c] = key
        u = key ^ INT_MIN
        for g in range(kc // GROUP_KEYS):
            words = _bit_transpose32([u[g * GROUP_KEYS + SUBLANES * j:g * GROUP_KEYS + SUBLANES * (j + 1), :]
                                      for j in range(32)])
            for p in range(32):
                planes_ref[p, c * (kc // GROUP_KEYS) + g] = words[p]
        return carry

    lax.fori_loop(0, n_kc, score_body, 0)

    n_groups = planes_ref.shape[1]
    n_live = n_kc * (kc // GROUP_KEYS)
    for g in range(n_groups):
        live_ref[g] = jnp.full((SUBLANES, qb), jnp.where(g < n_live, -1, 0), I32)
    kf = jnp.full((1, qb), float(topk), F32)

    def popcount_rows(words):
        cnt = lax.population_count(words)
        tot = cnt[0]
        for g in range(1, n_groups):
            tot = tot + cnt[g]
        return jnp.sum(tot.astype(F32), axis=0, keepdims=True)

    def bit_body(it, carry):
        n_gt, prefix = carry
        p = 31 - it
        plane = planes_ref[p]
        live = live_ref[...]
        n_one = popcount_rows(live & plane)
        take = (n_gt + n_one) >= kf
        live_ref[...] = live & (plane ^ jnp.where(take, 0, -1)[None])
        return jnp.where(take, n_gt, n_gt + n_one), prefix | jnp.where(take, jnp.left_shift(jnp.int32(1), p), 0)

    n_gt, prefix = lax.fori_loop(0, 32, bit_body, (jnp.zeros((1, qb), F32), jnp.zeros((1, qb), I32)))
    ans = prefix ^ INT_MIN
    need = kf - n_gt
    cnt_ans = n_gt + popcount_rows(live_ref[...])

    tie = (ans > INT_MIN) & (cnt_ans > kf)
    any_tie = jnp.max(jnp.where(tie, 1.0, 0.0)) > 0.0

    @pl.when(any_tie)
    def _():
        ansb = jnp.broadcast_to(ans, (kc, qb))
        tieb = jnp.broadcast_to(jnp.where(tie, 1.0, 0.0), (kc, qb)) > 0.0
        lower = (lax.broadcasted_iota(I32, (kc, kc), 0) >= lax.broadcasted_iota(I32, (kc, kc), 1))
        lower = jnp.where(lower, 1.0, 0.0).astype(BF16)

        def tie_body(c, run):
            blk = keys_ref[c]
            eq = (blk == ansb) & tieb
            eqf = jnp.where(eq, 1.0, 0.0).astype(BF16)
            pref = jnp.dot(lower, eqf, preferred_element_type=F32) + run
            keys_ref[c] = jnp.where(eq & (pref > need), INT_MIN, blk)
            return pref[kc - 1:kc, :]

        lax.fori_loop(0, n_kc, tie_body, jnp.zeros((1, qb), F32))

    thr = jnp.broadcast_to(jnp.maximum(ans, INT_MIN + 1), (kc, qb))

    q = q_ref[0]
    qg = [jnp.concatenate([q[:, (g * rep + r) * LANES:(g * rep + r + 1) * LANES] for r in range(rep)], axis=0)
          for g in range(ATT_KV)]

    last_chunk = keys_ref.shape[0] - 1

    def att_body(j, carry):
        carry = list(carry)

        def prep(u):
            cu = 2 * j + u
            cc = jnp.minimum(cu, last_chunk)
            bias = jnp.where(keys_ref[cc] >= thr, 0.0, NEG_BIG)
            bias = jnp.where(cu < n_kc, bias, NEG_BIG)
            keys_c = pl.ds(pl.multiple_of(cc * kc, kc), kc)
            return jnp.concatenate([bias] * rep, axis=1), keys_c

        def scores(g, bias, keys_c):
            kg = k_ref[0, keys_c, g * LANES:(g + 1) * LANES]
            return lax.dot_general(kg, qg[g], _NT, preferred_element_type=F32) + bias

        def softmax(g, st):
            m_old = carry[2 * g]
            m_new = jnp.maximum(m_old, jnp.max(st, axis=0, keepdims=True))
            carry[2 * g] = m_new
            return jnp.exp2(m_old - m_new), jnp.exp2(st - m_new).astype(BF16)

        def accumulate(g, alpha, pt, keys_c):
            pv = jnp.dot(vt_ref[0, g * LANES:(g + 1) * LANES, keys_c], pt, preferred_element_type=F32)
            carry[2 * g + 1] = carry[2 * g + 1] * alpha + pv

        bias_a, keys_a = prep(0)
        bias_b, keys_b = prep(1)
        st0a = scores(0, bias_a, keys_a)
        st1a = scores(1, bias_a, keys_a)
        al0a, p0a = softmax(0, st0a)
        st0b = scores(0, bias_b, keys_b)
        accumulate(0, al0a, p0a, keys_a)
        al1a, p1a = softmax(1, st1a)
        st1b = scores(1, bias_b, keys_b)
        accumulate(1, al1a, p1a, keys_a)
        al0b, p0b = softmax(0, st0b)
        accumulate(0, al0b, p0b, keys_b)
        al1b, p1b = softmax(1, st1b)
        accumulate(1, al1b, p1b, keys_b)
        return tuple(carry)

    init = []
    for g in range(ATT_KV):
        init += [jnp.full((1, rep * qb), NEG_BIG, F32), jnp.zeros((LANES, rep * qb), F32)]
    res = lax.fori_loop(0, (n_kc + 1) // 2, att_body, tuple(init))

    outs = []
    for g in range(ATT_KV):
        acc = res[2 * g + 1]
        ot = acc[:ATT_DH] / acc[ATT_DH:ATT_DH + 1]
        outs += [ot[:, r * qb:(r + 1) * qb] for r in range(rep)]
    o_ref[0] = jnp.concatenate(outs, axis=0).T.astype(o_ref.dtype)


def dsa_attention(q, qi, wi, k, ki, vt, b, s):
    kc = min(DSA_KC, s)
    topk = min(INDEX_TOPK, s // 4)
    qblk = lambda w: pl.BlockSpec((1, Q_BLOCK, w), lambda bi, i: (bi, i, 0))
    full = lambda w: pl.BlockSpec((1, s, w), lambda bi, i: (bi, 0, 0))
    out = pl.pallas_call(
        functools.partial(_dsa_kernel, kc=kc, topk=topk),
        grid=(b, s // Q_BLOCK),
        in_specs=[qblk(1024), qblk(512), qblk(128), full(256), full(128),
                  pl.BlockSpec((1, 256, s), lambda bi, i: (bi, 0, 0))],
        out_specs=qblk(BRANCH),
        out_shape=jax.ShapeDtypeStruct((b, s, BRANCH), BF16),
        scratch_shapes=[pltpu.VMEM((s // kc, kc, Q_BLOCK), I32),
                        pltpu.VMEM((32, s // GROUP_KEYS, SUBLANES, Q_BLOCK), I32),
                        pltpu.VMEM((s // GROUP_KEYS, SUBLANES, Q_BLOCK), I32)],
        compiler_params=_cp(("parallel", "arbitrary")),
        name="dsa_attention",
    )(q, qi, wi, k, ki, vt)
    return out.reshape(b * s, BRANCH)


def _rwkv_kernel(pb_ref, mu_ref, vec_ref, lora_ref, bones_ref, o_ref,
                 carry_ref, state_ref, r_s, k_s, v_s, kk_s, b_s, ld_s, y_s, *, exact):
    tt = pb_ref.shape[1]
    c_len = RW_C
    t_idx = pl.program_id(1)

    @pl.when(t_idx == 0)
    def _():
        carry_ref[...] = jnp.zeros_like(carry_ref)
        state_ref[...] = jnp.zeros_like(state_ref)

    p = pb_ref[0].astype(F32)
    prev = pltpu.roll(p, 1, axis=0)
    first = lax.broadcasted_iota(I32, p.shape, 0) == 0
    prev = jnp.where(first, jnp.broadcast_to(carry_ref[7:8, :], p.shape), prev)
    carry_ref[...] = p[tt - 8:, :]
    p = p + (prev - p) * mu_ref[...]

    w0, a0, k_k, k_a = vec_ref[0:1, :], vec_ref[1:2, :], vec_ref[2:3, :], vec_ref[3:4, :]
    r_k, ln_g, ln_b = vec_ref[4:5, :], vec_ref[5:6, :], vec_ref[6:7, :]
    bones = bones_ref[...]

    r = p[:, 0:512]
    k = p[:, 512:1024]
    v = p[:, 1024:1536]
    lo = p[:, 1536:1792]
    wl = _dot(jnp.tanh(lo), lora_ref[0])
    al = _dot(lo, lora_ref[1])
    g = _dot(jax.nn.sigmoid(lo), lora_ref[2])
    z = -(w0 + wl)
    w = -(jnp.maximum(z, 0.0) + jnp.log(1.0 + jnp.exp(-jnp.abs(z)))) - 0.5
    a = jax.nn.sigmoid(a0 + al)
    kk = k * k_k
    ssq = _dot_split(kk * kk, bones)
    kk = kk / jnp.maximum(jnp.sqrt(ssq), 1e-12)
    k = k * (1.0 + (a - 1.0) * k_a)
    r_s[...] = r
    k_s[...] = k
    v_s[...] = v
    kk_s[...] = kk
    b_s[...] = kk * a
    ld_s[...] = -jnp.exp(w)

    ii = lax.broadcasted_iota(I32, (c_len, c_len), 0)
    jj = lax.broadcasted_iota(I32, (c_len, c_len), 1)
    strict, incl = ii > jj, ii >= jj
    eye = jnp.where(ii == jj, 1.0, 0.0)
    tri = jnp.where(incl, 1.0, 0.0).astype(BF16)
    blk_masks = {1 << sh: (ii >> sh) == (jj >> sh) for sh in (3, 4, 5, 6)}

    dot = functools.partial(_dot, exact=exact)
    off_masks = [blk_masks[sz] & jnp.logical_not(blk_masks[sz // 2]) for sz in (16, 32, 64)]

    def chunk_body(ci, carry):
        probs = []
        for u in range(RW_PAR):
            rows = pl.ds(pl.multiple_of((ci * RW_PAR + u) * c_len, c_len), c_len)
            ld = ld_s[rows, :]
            cum = _dot_split(ld, tri, left=True)
            cum_end = cum[c_len - 1:c_len, :]
            e_inv, e_end = jnp.exp(-cum), jnp.exp(cum_end - cum)
            kc_, bc_ = k_s[rows, :], b_s[rows, :]
            kkt = kk_s[rows, :] * jnp.exp(cum - ld)
            rt = r_s[rows, :] * jnp.exp(cum)
            kd, bd = kc_ * e_inv, bc_ * e_inv
            kh, bh = kc_ * e_end, bc_ * e_end
            vc = v_s[rows, :]
            e_last = jnp.exp(cum_end)
            for h in range(RW_H):
                sl = slice(h * RW_N, (h + 1) * RW_N)
                probs.append(dict(rows=rows, sl=sl, h=h, kkt=kkt[:, sl], rt=rt[:, sl], kd=kd[:, sl], bd=bd[:, sl],
                                  kh=kh[:, sl], bh=bh[:, sl], v=vc[:, sl], e_last=e_last[:, sl]))
        for p in probs:
            big = dot(jnp.concatenate([p["kkt"], p["rt"]], axis=0),
                      jnp.concatenate([p["bd"], p["kd"]], axis=0), _NT)
            p["m_b"] = jnp.where(strict, big[:c_len, :c_len], 0.0)
            p["m_k"] = jnp.where(strict, big[:c_len, c_len:], 0.0)
            p["p_b"] = jnp.where(incl, big[c_len:, :c_len], 0.0)
            p["p_k"] = jnp.where(incl, big[c_len:, c_len:], 0.0)
            p["n8"] = jnp.where(blk_masks[8], p["m_b"], 0.0)
        for p in probs:
            p["n2"] = dot(p["n8"], p["n8"])
            p["mkv"] = dot(p["m_k"], p["v"])
        for p in probs:
            p["n4"] = dot(p["n2"], p["n2"])
            p["t"] = dot(eye - p["n8"], eye + p["n2"])
        for p in probs:
            p["t"] = dot(p["t"], eye + p["n4"])
        for om in off_masks:
            for p in probs:
                p["a"] = dot(p["t"], jnp.where(om, p["m_b"], 0.0))
            for p in probs:
                p["t"] = p["t"] - dot(p["a"], p["t"])
        for p in probs:
            p["wu"] = dot(p["t"], jnp.concatenate([p["kkt"], p["mkv"]], axis=1))
            p["pkv"] = dot(p["p_k"], p["v"])
        for p in probs:
            pwu = dot(p["p_b"], p["wu"])
            p["y0"] = p["pkv"] - pwu[:, RW_N:]
            p["gm"] = p["rt"] - pwu[:, :RW_N]
            p["bw"] = dot(p["bh"], p["wu"][:, :RW_N], _TN)
            p["psi_t"] = dot(jnp.concatenate([p["v"], p["wu"][:, RW_N:]], axis=0),
                             jnp.concatenate([p["kh"], -p["bh"]], axis=0), _TN)
        for p in probs:
            st = state_ref[p["h"]]
            y_s[p["rows"], p["sl"]] = p["y0"] + _dot(p["gm"], st, _NT, exact=True)
            state_ref[p["h"]] = st * p["e_last"] - _dot(st, p["bw"], _NT, exact=True) + p["psi_t"]
        return carry

    lax.fori_loop(0, tt // (c_len * RW_PAR), chunk_body, 0)

    y = y_s[...]
    r, k, v = r_s[...], k_s[...], v_s[...]
    mean = _dot_split(y, bones) * (1.0 / RW_N)
    yc = y - mean
    var = _dot_split(yc * yc, bones) * (1.0 / RW_N)
    y = yc * lax.rsqrt(var + RW_GN_EPS) * ln_g + ln_b
    y = y + _dot_split(r * k * r_k, bones) * v
    o_ref[0] = (y * g).astype(o_ref.dtype)


def rwkv_mix(pb, mu, vecs, lora, b, s, exact=False):
    tt = min(RW_TT, s)
    bones = (jnp.arange(BRANCH)[:, None] // RW_N == jnp.arange(BRANCH)[None, :] // RW_N).astype(BF16)
    const = lambda shape: pl.BlockSpec(shape, lambda bi, ti: (0,) * len(shape))
    sq = lambda: pltpu.VMEM((tt, BRANCH), F32)
    out = pl.pallas_call(
        functools.partial(_rwkv_kernel, exact=exact),
        grid=(b, s // tt),
        in_specs=[pl.BlockSpec((1, tt, PB_W), lambda bi, ti: (bi, ti, 0)),
                  const((1, PB_W)), const((8, BRANCH)), const((3, 256, BRANCH)), const((BRANCH, BRANCH))],
        out_specs=pl.BlockSpec((1, tt, BRANCH), lambda bi, ti: (bi, ti, 0)),
        out_shape=jax.ShapeDtypeStruct((b, s, BRANCH), BF16),
        scratch_shapes=[pltpu.VMEM((8, PB_W), F32), pltpu.VMEM((RW_H, RW_N, RW_N), F32),
                        sq(), sq(), sq(), sq(), sq(), sq(), sq()],
        compiler_params=_cp(("parallel", "arbitrary")),
        name="rwkv_mix",
    )(pb.reshape(b, s, PB_W), mu, vecs, lora, bones)
    return out.reshape(b * s, BRANCH)


def _pool_kernel(pc_ref, w_ref, sc_ref, o_ref, carry_ref):
    ts = pc_ref.shape[1]
    t_idx = pl.program_id(1)

    @pl.when(t_idx == 0)
    def _():
        carry_ref[...] = jnp.zeros_like(carry_ref)

    x = pc_ref[0].astype(F32)
    ext = jnp.concatenate([carry_ref[...], x], axis=0)
    carry_ref[...] = x[ts - 16:, :]
    n = ts + 16
    s2 = ext + pltpu.roll(ext, 1, axis=0)
    s4 = s2[:, 128:] + pltpu.roll(s2[:, 128:], 2, axis=0)
    s8 = s4[:, 128:] + pltpu.roll(s4[:, 128:], 4, axis=0)
    s16 = s8[:, 128:] + pltpu.roll(s8[:, 128:], 8, axis=0)
    sums = (s2[16:, 0:128], s4[16:, 0:128], s8[16:, 0:128], s16[16:, 0:128])
    step = (lax.broadcasted_iota(I32, (ts, POOL_GD), 0) + t_idx * ts + 1).astype(F32)
    outs = []
    for gi, win in enumerate(POOL_WINDOWS):
        mean = sums[gi] / jnp.minimum(step, float(win))
        pooled = mean - x[:, gi * POOL_GD:(gi + 1) * POOL_GD]
        outs.append(jnp.dot(pooled.astype(BF16), w_ref[gi], preferred_element_type=F32))
    del n
    o_ref[0] = (jnp.concatenate(outs, axis=1) * sc_ref[...]).astype(o_ref.dtype)


def pool_mix(pc, pool_w, pool_scale, b, s):
    ts = min(POOL_TS, s)
    out = pl.pallas_call(
        _pool_kernel,
        grid=(b, s // ts),
        in_specs=[pl.BlockSpec((1, ts, BRANCH), lambda bi, ti: (bi, ti, 0)),
                  pl.BlockSpec((4, POOL_GD, POOL_GD), lambda bi, ti: (0, 0, 0)),
                  pl.BlockSpec((1, BRANCH), lambda bi, ti: (0, 0))],
        out_specs=pl.BlockSpec((1, ts, BRANCH), lambda bi, ti: (bi, ti, 0)),
        out_shape=jax.ShapeDtypeStruct((b, s, BRANCH), BF16),
        scratch_shapes=[pltpu.VMEM((16, BRANCH), F32)],
        compiler_params=_cp(("parallel", "arbitrary")),
        name="pool_mix",
    )(pc.reshape(b, s, BRANCH), pool_w, pool_scale)
    return out.reshape(b * s, BRANCH)


def _ret_kernel(pd_ref, cos_ref, sin_ref, gn_ref, bones_ref, o_ref, state_ref, o_s):
    c_len = pd_ref.shape[1]
    t_idx = pl.program_id(1)

    @pl.when(t_idx == 0)
    def _():
        state_ref[...] = jnp.zeros_like(state_ref)

    cos, sin = cos_ref[0], sin_ref[0]
    q = _rope_apply(pd_ref[0, :, 0:512].astype(F32), cos, sin, RET_DK)
    k = _rope_apply(pd_ref[0, :, 512:1024].astype(F32), cos, sin, RET_DK) * (RET_DK ** -0.5)
    v = pd_ref[0, :, 1024:1536]
    gate = pd_ref[0, :, 1536:2048].astype(F32)

    ii = lax.broadcasted_iota(I32, (c_len, c_len), 0)
    jj = lax.broadcasted_iota(I32, (c_len, c_len), 1)
    diff = (ii - jj).astype(F32)
    pos = lax.broadcasted_iota(I32, (c_len, 1), 0).astype(F32)
    for h in range(RET_H):
        log_gamma = math.log(1.0 - 2.0 ** (-5.0 - h))
        intra = jnp.where(diff >= 0, jnp.exp(jnp.maximum(diff, 0.0) * log_gamma), 0.0)
        q_decay = jnp.exp((pos + 1.0) * log_gamma)
        k_decay = jnp.exp((c_len - 1.0 - pos) * log_gamma)
        c_decay = math.exp(c_len * log_gamma)
        sl = slice(h * 64, (h + 1) * 64)
        qh, kh, vh = q[:, sl], k[:, sl], v[:, sl]
        st = state_ref[h]
        sc = _dot(qh, kh, _NT) * intra
        o = _dot(sc, vh) + _dot(qh, st) * q_decay
        state_ref[h] = st * c_decay + _dot(kh * k_decay, vh, _TN)
        o_s[:, sl] = o

    o = o_s[...]
    bones = bones_ref[...]
    mean = _dot_split(o, bones) * (1.0 / RET_DV)
    oc = o - mean
    var = _dot_split(oc * oc, bones) * (1.0 / RET_DV)
    o = oc * lax.rsqrt(var + RET_GN_EPS) * gn_ref[...]
    o_ref[0] = (jax.nn.silu(gate) * o).astype(o_ref.dtype)


def ret_mix(pd, cos_r, sin_r, gn_g, b, s):
    c_len = min(RET_C, s)
    bones = (jnp.arange(BRANCH)[:, None] // RET_DV == jnp.arange(BRANCH)[None, :] // RET_DV).astype(BF16)
    out = pl.pallas_call(
        _ret_kernel,
        grid=(b, s // c_len),
        in_specs=[pl.BlockSpec((1, c_len, PD_W), lambda bi, ti: (bi, ti, 0)),
                  pl.BlockSpec((1, c_len, LANES), lambda bi, ti: (bi, ti, 0)),
                  pl.BlockSpec((1, c_len, LANES), lambda bi, ti: (bi, ti, 0)),
                  pl.BlockSpec((1, BRANCH), lambda bi, ti: (0, 0)),
                  pl.BlockSpec((BRANCH, BRANCH), lambda bi, ti: (0, 0))],
        out_specs=pl.BlockSpec((1, c_len, BRANCH), lambda bi, ti: (bi, ti, 0)),
        out_shape=jax.ShapeDtypeStruct((b, s, BRANCH), BF16),
        scratch_shapes=[pltpu.VMEM((RET_H, 64, RET_DV), F32), pltpu.VMEM((c_len, BRANCH), F32)],
        compiler_params=_cp(("parallel", "arbitrary")),
        name="ret_mix",
    )(pd.reshape(b, s, PD_W), cos_r.reshape(b, s, LANES), sin_r.reshape(b, s, LANES), gn_g, bones)
    return out.reshape(b * s, BRANCH)


def _merge_kernel(x_ref, pg_ref, oa_ref, ob_ref, oc_ref, od_ref, gb_ref, wb_ref, wo_ref, o_ref):
    merged = None
    for i, br in enumerate((oa_ref, ob_ref, oc_ref, od_ref)):
        gate = jax.nn.sigmoid(pg_ref[:, i * D_MODEL:(i + 1) * D_MODEL].astype(F32) + gb_ref[i:i + 1, :])
        term = gate * jnp.dot(br[...], wb_ref[i], preferred_element_type=F32)
        merged = term if merged is None else merged + term
    o_ref[...] = x_ref[...] + jnp.dot(merged.astype(BF16), wo_ref[...], preferred_element_type=F32)


def merge_out(x, pg, o_a, o_b, o_c, o_d, gate_b, w_branch, w_out):
    t = x.shape[0]
    tm = min(MERGE_TM, t)
    row = lambda w: pl.BlockSpec((tm, w), lambda i: (i, 0))
    return pl.pallas_call(
        _merge_kernel,
        grid=(t // tm,),
        in_specs=[row(D_MODEL), row(4 * D_MODEL), row(BRANCH), row(BRANCH), row(BRANCH), row(BRANCH),
                  pl.BlockSpec((4, D_MODEL), lambda i: (0, 0)),
                  pl.BlockSpec((4, BRANCH, D_MODEL), lambda i: (0, 0, 0)),
                  pl.BlockSpec((D_MODEL, D_MODEL), lambda i: (0, 0))],
        out_specs=row(D_MODEL),
        out_shape=jax.ShapeDtypeStruct((t, D_MODEL), F32),
        compiler_params=_cp(("parallel",)),
        name="merge_out",
    )(x, pg, o_a, o_b, o_c, o_d, gate_b, w_branch, w_out)


def _mlp_kernel(x_ref, g_ref, up_ref, down_ref, fg_ref, o_ref, h_ref, acc_ref, *, final_norm):
    j = pl.program_id(1)

    @pl.when(j == 0)
    def _():
        x = x_ref[...]
        ms = jnp.mean(x * x, axis=-1, keepdims=True)
        h_ref[...] = (x * lax.rsqrt(ms + NORM_EPS) * g_ref[...]).astype(BF16)
        acc_ref[...] = x

    u = jnp.dot(h_ref[...], up_ref[...], preferred_element_type=F32)
    u = jnp.square(jnp.maximum(u, 0.0))
    acc_ref[...] += jnp.dot(u.astype(BF16), down_ref[...], preferred_element_type=F32)

    @pl.when(j == pl.num_programs(1) - 1)
    def _():
        y = acc_ref[...]
        if final_norm:
            ms = jnp.mean(y * y, axis=-1, keepdims=True)
            y = y * lax.rsqrt(ms + NORM_EPS) * fg_ref[...]
        o_ref[...] = y


def mlp_block(x, g, up, down, final_g, final_norm):
    t = x.shape[0]
    tm = min(MLP_TM, t)
    return pl.pallas_call(
        functools.partial(_mlp_kernel, final_norm=final_norm),
        grid=(t // tm, D_FF // MLP_FF),
        in_specs=[pl.BlockSpec((tm, D_MODEL), lambda i, j: (i, 0)),
                  pl.BlockSpec((1, D_MODEL), lambda i, j: (0, 0)),
                  pl.BlockSpec((D_MODEL, MLP_FF), lambda i, j: (0, j)),
                  pl.BlockSpec((MLP_FF, D_MODEL), lambda i, j: (j, 0)),
                  pl.BlockSpec((1, D_MODEL), lambda i, j: (0, 0))],
        out_specs=pl.BlockSpec((tm, D_MODEL), lambda i, j: (i, 0)),
        out_shape=jax.ShapeDtypeStruct((t, D_MODEL), F32),
        scratch_shapes=[pltpu.VMEM((tm, D_MODEL), BF16), pltpu.VMEM((tm, D_MODEL), F32)],
        compiler_params=_cp(("parallel", "arbitrary")),
        name="mlp_block",
    )(x, g, up, down, final_g)


def _partner_cols(w, heads, dh, rot):
    lead = w.shape[:-1]
    w3 = w.reshape(*lead, heads, dh)
    half = rot // 2
    part = jnp.concatenate([-w3[..., half:rot], w3[..., :half], jnp.zeros_like(w3[..., rot:])], axis=-1)
    return w3, part


def _interleave(w, heads, dh, rot):
    w3, part = _partner_cols(w, heads, dh, rot)
    return jnp.concatenate([w3, part], axis=-1).reshape(*w.shape[:-1], heads * 2 * dh)


def _prep_weights(w_in, rwkv_mu):
    n_a = 512 + 128 + 128 + 256 + 64 + 4
    n_b = 1696
    off_b, off_c, off_d = n_a, n_a + n_b, n_a + n_b + 512
    off_g = off_d + 1536
    wa = w_in[..., :n_a]
    q, k, v = wa[..., 0:512], wa[..., 512:640], wa[..., 640:768]
    qi, ki, wi = wa[..., 768:1024], wa[..., 1024:1088], wa[..., 1088:1092]
    pad = lambda w, n: jnp.pad(w, [(0, 0)] * (w.ndim - 1) + [(0, n - w.shape[-1])])
    w_a = jnp.concatenate([_interleave(q, ATT_HEADS, ATT_DH, ROT_DIM), _interleave(qi, IDX_HEADS, IDX_DH, ROT_DIM),
                           _interleave(k, ATT_KV, ATT_DH, ROT_DIM), _interleave(ki, 1, IDX_DH, ROT_DIM),
                           v, pad(wi, 128)], axis=-1)
    w_a = pad(w_a, PA_W)
    wb = w_in[..., off_b:off_c]
    perm = lambda z: jnp.concatenate([z[..., 0:512], z[..., 544:1056], z[..., 1056:1568],
                                      z[..., 512:544], z[..., 1568:1600], z[..., 1600:1696]], axis=-1)
    w_b = pad(perm(wb), PB_W)
    mu = pad(perm(rwkv_mu), PB_W)
    w_c = w_in[..., off_c:off_d]
    wd = w_in[..., off_d:off_g]
    rq, rk, rv, rg = wd[..., 0:256], wd[..., 256:512], wd[..., 512:1024], wd[..., 1024:1536]
    w_d = jnp.concatenate([_interleave(rq, RET_H, RET_DK, RET_DK), _interleave(rk, RET_H, RET_DK, RET_DK), rv, rg],
                          axis=-1)
    w_g = w_in[..., off_g:]
    cast = lambda w: w.astype(BF16)
    return cast(w_a), cast(w_b), cast(w_c), cast(w_d), cast(w_g), mu


def kernel(x, positions, attn_norm_g, w_in, rwkv_mu, rwkv_w0, rwkv_w2, rwkv_a0, rwkv_a2, rwkv_g2,
           rwkv_k_k, rwkv_k_a, rwkv_r_k, rwkv_ln_g, rwkv_ln_b, pool_w, pool_scale, ret_gn_g,
           gate_b, w_branch, w_out, mlp_norm_g, mlp_up, mlp_down, final_norm_g):
    b, s, d = x.shape
    depth = w_in.shape[0]
    t = b * s
    w_a, w_b, w_c, w_d, w_g, mu = _prep_weights(w_in, rwkv_mu)
    zeros = jnp.zeros_like(rwkv_w0)
    vecs = jnp.stack([rwkv_w0, rwkv_a0, rwkv_k_k, rwkv_k_a, rwkv_r_k, rwkv_ln_g, rwkv_ln_b, zeros], axis=1)
    lora = jnp.zeros((depth, 3, 256, BRANCH), F32)
    lora = lora.at[:, 0, 0:RW_DL].set(rwkv_w2)
    lora = lora.at[:, 1, RW_DL:RW_DL + RW_AL].set(rwkv_a2)
    lora = lora.at[:, 2, RW_DL + RW_AL:RW_DL + RW_AL + RW_GL].set(rwkv_g2)
    lora = lora.astype(BF16)
    pool_w16, w_branch16, w_out16 = pool_w.astype(BF16), w_branch.astype(BF16), w_out.astype(BF16)
    up16, down16 = mlp_up.astype(BF16), mlp_down.astype(BF16)

    cos_a, sin_a, cos_r, sin_r = rope_tables(positions)
    xf = x.reshape(t, d)
    for l in range(depth):
        g = attn_norm_g[l][None]
        pa = norm_matmul(xf, g, w_a[l], F32)
        pb = norm_matmul(xf, g, w_b[l], BF16)
        pc = norm_matmul(xf, g, w_c[l], BF16)
        pd = norm_matmul(xf, g, w_d[l], BF16)
        pg = norm_matmul(xf, g, w_g[l], BF16)
        q, qi, k, ki, vt, wi = dsa_prep(pa, cos_a, sin_a, b, s)
        o_a = dsa_attention(q, qi, wi, k, ki, vt, b, s)
        o_b = rwkv_mix(pb, mu[l][None], vecs[l], lora[l], b, s)
        o_c = pool_mix(pc, pool_w16[l], pool_scale[l][None], b, s)
        o_d = ret_mix(pd, cos_r, sin_r, ret_gn_g[l][None], b, s)
        xf = merge_out(xf, pg, o_a, o_b, o_c, o_d, gate_b[l], w_branch16[l], w_out16[l])
        xf = mlp_block(xf, mlp_norm_g[l][None], up16[l], down16[l], final_norm_g[None], l == depth - 1)
    return xf.reshape(b, s, d)
```

```python
import functools
import math

import jax
import jax.numpy as jnp
from jax import lax
from jax.experimental import pallas as pl
from jax.experimental.pallas import tpu as pltpu

F32 = jnp.float32
BF16 = jnp.bfloat16
I32 = jnp.int32

D_MODEL = 1024
BRANCH = 512
ATT_HEADS, ATT_KV, ATT_DH, ROT_DIM = 8, 2, 64, 16
ROPE_THETA = 500000.0
IDX_HEADS, IDX_DH, INDEX_TOPK, Q_BLOCK = 4, 64, 256, 128
RW_N, RW_H = 64, 8
RW_DL, RW_AL, RW_GL = 32, 32, 96
RW_GN_EPS = 64e-5
POOL_WINDOWS = (2, 4, 8, 16)
POOL_GD = 128
RET_H, RET_DK, RET_DV = 8, 32, 64
RET_THETA = 10000.0
RET_GN_EPS = 1e-6
D_FF = 4096
NORM_EPS = 1e-5

LANES = 128
VMEM_LIMIT = 56 * 1024 * 1024

PROJ_TM = 1024
DSA_KC = 512
RW_C = 64
RW_TT = 512
RW_PAR = 4
POOL_TS = 512
RET_C = 256
MERGE_TM = 512
MLP_TM = 512
MLP_FF = 1024

INT_MIN = -2 ** 31
SUBLANES = 8
GROUP_KEYS = 32 * SUBLANES
LOG2E = 1.4426950408889634
NEG_BIG = -1e30

PA_W = 2304
PA_Q, PA_QI, PA_K, PA_KI, PA_V, PA_WI = 0, 1024, 1536, 1792, 1920, 2048
PB_W = 1792
PD_W = 2048


def _cp(sem, vmem=VMEM_LIMIT):
    return pltpu.CompilerParams(dimension_semantics=sem, vmem_limit_bytes=vmem)


def _dot(a, b, dims=(((1,), (0,)), ((), ())), exact=False):
    if exact:
        return lax.dot_general(a.astype(F32), b.astype(F32), dims, precision=lax.Precision.HIGHEST,
                               preferred_element_type=F32)
    return lax.dot_general(a.astype(BF16), b.astype(BF16), dims, preferred_element_type=F32)


_NN = (((1,), (0,)), ((), ()))
_NT = (((1,), (1,)), ((), ()))
_TN = (((0,), (0,)), ((), ()))


def _dot_split(x, ones_rhs, dims=_NN, left=False):
    hi = x.astype(BF16)
    lo = (x - hi.astype(F32)).astype(BF16)
    if left:
        return (lax.dot_general(ones_rhs, hi, dims, preferred_element_type=F32)
                + lax.dot_general(ones_rhs, lo, dims, preferred_element_type=F32))
    return (lax.dot_general(hi, ones_rhs, dims, preferred_element_type=F32)
            + lax.dot_general(lo, ones_rhs, dims, preferred_element_type=F32))


def _norm_matmul_kernel(x_ref, g_ref, w_ref, o_ref, h_ref):
    @pl.when(pl.program_id(1) == 0)
    def _():
        x = x_ref[...]
        ms = jnp.mean(x * x, axis=-1, keepdims=True)
        h_ref[...] = (x * lax.rsqrt(ms + NORM_EPS) * g_ref[...]).astype(BF16)

    o_ref[...] = jnp.dot(h_ref[...], w_ref[...], preferred_element_type=F32).astype(o_ref.dtype)


def _pick_tn(n):
    for tn in (1024, 896, 768, 640, 512, 384, 256, 128):
        if n % tn == 0:
            return tn
    raise ValueError(n)


def norm_matmul(x, g, w, out_dtype):
    t, d = x.shape
    n = w.shape[1]
    tm, tn = min(PROJ_TM, t), _pick_tn(n)
    return pl.pallas_call(
        _norm_matmul_kernel,
        grid=(t // tm, n // tn),
        in_specs=[pl.BlockSpec((tm, d), lambda i, j: (i, 0)),
                  pl.BlockSpec((1, d), lambda i, j: (0, 0)),
                  pl.BlockSpec((d, tn), lambda i, j: (0, j))],
        out_specs=pl.BlockSpec((tm, tn), lambda i, j: (i, j)),
        out_shape=jax.ShapeDtypeStruct((t, n), out_dtype),
        scratch_shapes=[pltpu.VMEM((tm, d), BF16)],
        compiler_params=_cp(("parallel", "arbitrary")),
        name="norm_matmul",
    )(x, g, w)


def _rope_table_kernel(pos_ref, fa_ref, fr_ref, ma_ref, mr_ref, ca_ref, sa_ref, cr_ref, sr_ref):
    pos = pos_ref[...].astype(F32)
    ang_a = pos * fa_ref[...]
    ca_ref[...] = jnp.cos(ang_a) * ma_ref[...]
    sa_ref[...] = jnp.sin(ang_a) * ma_ref[...]
    ang_r = pos * fr_ref[...]
    cr_ref[...] = jnp.cos(ang_r) * mr_ref[...]
    sr_ref[...] = jnp.sin(ang_r) * mr_ref[...]


def rope_tables(positions):
    t = positions.size
    pos = positions.reshape(t, 1)
    inv_a = ROPE_THETA ** (-jnp.arange(0, ROT_DIM, 2, dtype=F32) / ROT_DIM)
    fa = jnp.concatenate([inv_a, inv_a, jnp.zeros((LANES - ROT_DIM,), F32)])[None]
    ma = jnp.concatenate([jnp.ones((ATT_DH,), F32), jnp.zeros((LANES - ATT_DH,), F32)])[None]
    inv_r = 1.0 / (RET_THETA ** jnp.linspace(0.0, 1.0, RET_DK // 2, dtype=F32))
    fr64 = jnp.concatenate([inv_r, inv_r, jnp.zeros((RET_DK,), F32)])
    fr = jnp.concatenate([fr64, fr64])[None]
    mr64 = jnp.concatenate([jnp.ones((RET_DK,), F32), jnp.zeros((RET_DK,), F32)])
    mr = jnp.concatenate([mr64, mr64])[None]
    ts = min(1024, t)
    row = pl.BlockSpec((1, LANES), lambda i: (0, 0))
    tab = pl.BlockSpec((ts, LANES), lambda i: (i, 0))
    return pl.pallas_call(
        _rope_table_kernel,
        grid=(t // ts,),
        in_specs=[pl.BlockSpec((ts, 1), lambda i: (i, 0)), row, row, row, row],
        out_specs=[tab, tab, tab, tab],
        out_shape=[jax.ShapeDtypeStruct((t, LANES), F32)] * 4,
        compiler_params=_cp(("parallel",)),
        name="rope_tables",
    )(pos, fa, fr, ma, mr)


def _rope_apply(p, cos, sin, half_block):
    w = p.shape[-1]
    reps = w // cos.shape[-1]
    c = jnp.concatenate([cos] * reps, axis=-1) if reps > 1 else cos
    s = jnp.concatenate([sin] * reps, axis=-1) if reps > 1 else sin
    partner = pltpu.roll(p, w - half_block, axis=1)
    return p * c + partner * s


def _dsa_prep_kernel(pa_ref, cos_ref, sin_ref, q_ref, qi_ref, k_ref, ki_ref, vt_ref, wi_ref):
    cos, sin = cos_ref[0], sin_ref[0]
    q_scale = ATT_DH ** -0.5 * LOG2E
    q_ref[0] = (_rope_apply(pa_ref[0, :, PA_Q:PA_Q + 1024], cos, sin, ATT_DH) * q_scale).astype(BF16)
    qi_ref[0] = _rope_apply(pa_ref[0, :, PA_QI:PA_QI + 512], cos, sin, ATT_DH).astype(BF16)
    k_ref[0] = _rope_apply(pa_ref[0, :, PA_K:PA_K + 256], cos, sin, ATT_DH).astype(BF16)
    ki_ref[0] = _rope_apply(pa_ref[0, :, PA_KI:PA_KI + 128], cos, sin, ATT_DH).astype(BF16)
    vt = pa_ref[0, :, PA_V:PA_V + 128].T
    ones = jnp.ones((ATT_DH, vt.shape[1]), F32)
    vt_ref[0] = jnp.concatenate([vt[:ATT_DH], ones, vt[ATT_DH:], ones], axis=0).astype(BF16)
    wi_ref[0] = pa_ref[0, :, PA_WI:PA_WI + 128]


def dsa_prep(pa, cos_a, sin_a, b, s):
    tq = min(512, s)
    blk = lambda w: pl.BlockSpec((1, tq, w), lambda bi, i: (bi, i, 0))
    r3 = lambda a: a.reshape(b, s, a.shape[-1])
    shp = lambda w, dt: jax.ShapeDtypeStruct((b, s, w), dt)
    return pl.pallas_call(
        _dsa_prep_kernel,
        grid=(b, s // tq),
        in_specs=[blk(PA_W), blk(LANES), blk(LANES)],
        out_specs=[blk(1024), blk(512), blk(256), blk(128),
                   pl.BlockSpec((1, 256, tq), lambda bi, i: (bi, 0, i)), blk(128)],
        out_shape=[shp(1024, BF16), shp(512, BF16), shp(256, BF16), shp(128, BF16),
                   jax.ShapeDtypeStruct((b, 256, s), BF16), shp(128, F32)],
        compiler_params=_cp(("parallel", "parallel")),
        name="dsa_prep",
    )(r3(pa), r3(cos_a), r3(sin_a))


def _bit_transpose32(words):
    a = list(words)
    for sh, mask in ((16, 0x0000FFFF), (8, 0x00FF00FF), (4, 0x0F0F0F0F), (2, 0x33333333), (1, 0x55555555)):
        for k in range(32):
            if k & sh:
                continue
            t = (jnp.right_shift(a[k], sh) ^ a[k + sh]) & mask
            a[k + sh] = a[k + sh] ^ t
            a[k] = a[k] ^ jnp.left_shift(t, sh)
    return a


def _dsa_kernel(q_ref, qi_ref, wi_ref, k_ref, ki_ref, vt_ref, o_ref, keys_ref, planes_ref, live_ref, *, kc, topk):
    qb = Q_BLOCK
    i = pl.program_id(1)
    n_kc = (i * qb + qb + kc - 1) // kc
    rep = ATT_HEADS // ATT_KV

    @pl.when(i == 0)
    def _():
        keys_ref[...] = jnp.full(keys_ref.shape, INT_MIN, I32)
        planes_ref[...] = jnp.zeros(planes_ref.shape, I32)

    qi = qi_ref[0]
    qis = jnp.concatenate([qi[:, h * LANES:(h + 1) * LANES] for h in range(IDX_HEADS)], axis=0)
    wit = wi_ref[0].T
    w_all = jnp.concatenate([wit[h:h + 1, :] for h in range(IDX_HEADS)], axis=1)
    kpos = lax.broadcasted_iota(I32, (kc, qb), 0)
    qpos = lax.broadcasted_iota(I32, (kc, qb), 1) + i * qb

    def score_body(c, carry):
        kic = ki_ref[0, pl.ds(pl.multiple_of(c * kc, kc), kc), :]
        d = lax.dot_general(kic, qis, _NT, preferred_element_type=F32)
        d = jnp.maximum(d, 0.0) * w_all
        acc = d[:, 0:qb]
        for h in range(1, IDX_HEADS):
            acc = acc + d[:, h * qb:(h + 1) * qb]
        acc = jnp.where(acc == 0.0, 0.0, acc)
        bits = pltpu.bitcast(acc, I32)
        key = jnp.where(bits < 0, bits ^ 0x7FFFFFFF, bits)
        key = jnp.where(kpos + c * kc <= qpos, key, INT_MIN)
        keys_ref[c] = key
        u = key ^ INT_MIN
        for g in range(kc // GROUP_KEYS):
            words = _bit_transpose32([u[g * GROUP_KEYS + SUBLANES * j:g * GROUP_KEYS + SUBLANES * (j + 1), :]
                                      for j in range(32)])
            for p in range(32):
                planes_ref[p, c * (kc // GROUP_KEYS) + g] = words[p]
        return carry

    lax.fori_loop(0, n_kc, score_body, 0)

    n_groups = planes_ref.shape[1]
    n_live = n_kc * (kc // GROUP_KEYS)
    for g in range(n_groups):
        live_ref[g] = jnp.full((SUBLANES, qb), jnp.where(g < n_live, -1, 0), I32)
    kf = jnp.full((1, qb), float(topk), F32)

    def popcount_rows(words):
        cnt = lax.population_count(words)
        tot = cnt[0]
        for g in range(1, n_groups):
            tot = tot + cnt[g]
        return jnp.sum(tot.astype(F32), axis=0, keepdims=True)

    def bit_body(it, carry):
        n_gt, prefix = carry
        p = 31 - it
        plane = planes_ref[p]
        live = live_ref[...]
        n_one = popcount_rows(live & plane)
        take = (n_gt + n_one) >= kf
        live_ref[...] = live & (plane ^ jnp.where(take, 0, -1)[None])
        return jnp.where(take, n_gt, n_gt + n_one), prefix | jnp.where(take, jnp.left_shift(jnp.int32(1), p), 0)

    n_gt, prefix = lax.fori_loop(0, 32, bit_body, (jnp.zeros((1, qb), F32), jnp.zeros((1, qb), I32)))
    ans = prefix ^ INT_MIN
    need = kf - n_gt
    cnt_ans = n_gt + popcount_rows(live_ref[...])

    tie = (ans > INT_MIN) & (cnt_ans > kf)
    any_tie = jnp.max(jnp.where(tie, 1.0, 0.0)) > 0.0

    @pl.when(any_tie)
    def _():
        ansb = jnp.broadcast_to(ans, (kc, qb))
        tieb = jnp.broadcast_to(jnp.where(tie, 1.0, 0.0), (kc, qb)) > 0.0
        lower = (lax.broadcasted_iota(I32, (kc, kc), 0) >= lax.broadcasted_iota(I32, (kc, kc), 1))
        lower = jnp.where(lower, 1.0, 0.0).astype(BF16)

        def tie_body(c, run):
            blk = keys_ref[c]
            eq = (blk == ansb) & tieb
            eqf = jnp.where(eq, 1.0, 0.0).astype(BF16)
            pref = jnp.dot(lower, eqf, preferred_element_type=F32) + run
            keys_ref[c] = jnp.where(eq & (pref > need), INT_MIN, blk)
            return pref[kc - 1:kc, :]

        lax.fori_loop(0, n_kc, tie_body, jnp.zeros((1, qb), F32))

    thr = jnp.broadcast_to(jnp.maximum(ans, INT_MIN + 1), (kc, qb))

    q = q_ref[0]
    qg = [jnp.concatenate([q[:, (g * rep + r) * LANES:(g * rep + r + 1) * LANES] for r in range(rep)], axis=0)
          for g in range(ATT_KV)]

    last_chunk = keys_ref.shape[0] - 1

    def att_body(j, carry):
        carry = list(carry)

        def prep(u):
            cu = 2 * j + u
            cc = jnp.minimum(cu, last_chunk)
            bias = jnp.where(keys_ref[cc] >= thr, 0.0, NEG_BIG)
            bias = jnp.where(cu < n_kc, bias, NEG_BIG)
            keys_c = pl.ds(pl.multiple_of(cc * kc, kc), kc)
            return jnp.concatenate([bias] * rep, axis=1), keys_c

        def scores(g, bias, keys_c):
            kg = k_ref[0, keys_c, g * LANES:(g + 1) * LANES]
            return lax.dot_general(kg, qg[g], _NT, preferred_element_type=F32) + bias

        def softmax(g, st):
            m_old = carry[2 * g]
            m_new = jnp.maximum(m_old, jnp.max(st, axis=0, keepdims=True))
            carry[2 * g] = m_new
            return jnp.exp2(m_old - m_new), jnp.exp2(st - m_new).astype(BF16)

        def accumulate(g, alpha, pt, keys_c):
            pv = jnp.dot(vt_ref[0, g * LANES:(g + 1) * LANES, keys_c], pt, preferred_element_type=F32)
            carry[2 * g + 1] = carry[2 * g + 1] * alpha + pv

        bias_a, keys_a = prep(0)
        bias_b, keys_b = prep(1)
        st0a = scores(0, bias_a, keys_a)
        st1a = scores(1, bias_a, keys_a)
        al0a, p0a = softmax(0, st0a)
        st0b = scores(0, bias_b, keys_b)
        accumulate(0, al0a, p0a, keys_a)
        al1a, p1a = softmax(1, st1a)
        st1b = scores(1, bias_b, keys_b)
        accumulate(1, al1a, p1a, keys_a)
        al0b, p0b = softmax(0, st0b)
        accumulate(0, al0b, p0b, keys_b)
        al1b, p1b = softmax(1, st1b)
        accumulate(1, al1b, p1b, keys_b)
        return tuple(carry)

    init = []
    for g in range(ATT_KV):
        init += [jnp.full((1, rep * qb), NEG_BIG, F32), jnp.zeros((LANES, rep * qb), F32)]
    res = lax.fori_loop(0, (n_kc + 1) // 2, att_body, tuple(init))

    outs = []
    for g in range(ATT_KV):
        acc = res[2 * g + 1]
        ot = acc[:ATT_DH] / acc[ATT_DH:ATT_DH + 1]
        outs += [ot[:, r * qb:(r + 1) * qb] for r in range(rep)]
    o_ref[0] = jnp.concatenate(outs, axis=0).T.astype(o_ref.dtype)


def dsa_attention(q, qi, wi, k, ki, vt, b, s):
    kc = min(DSA_KC, s)
    topk = min(INDEX_TOPK, s // 4)
    qblk = lambda w: pl.BlockSpec((1, Q_BLOCK, w), lambda bi, i: (bi, i, 0))
    full = lambda w: pl.BlockSpec((1, s, w), lambda bi, i: (bi, 0, 0))
    out = pl.pallas_call(
        functools.partial(_dsa_kernel, kc=kc, topk=topk),
        grid=(b, s // Q_BLOCK),
        in_specs=[qblk(1024), qblk(512), qblk(128), full(256), full(128),
                  pl.BlockSpec((1, 256, s), lambda bi, i: (bi, 0, 0))],
        out_specs=qblk(BRANCH),
        out_shape=jax.ShapeDtypeStruct((b, s, BRANCH), BF16),
        scratch_shapes=[pltpu.VMEM((s // kc, kc, Q_BLOCK), I32),
                        pltpu.VMEM((32, s // GROUP_KEYS, SUBLANES, Q_BLOCK), I32),
                        pltpu.VMEM((s // GROUP_KEYS, SUBLANES, Q_BLOCK), I32)],
        compiler_params=_cp(("parallel", "arbitrary")),
        name="dsa_attention",
    )(q, qi, wi, k, ki, vt)
    return out.reshape(b * s, BRANCH)


def _rwkv_kernel(pb_ref, mu_ref, vec_ref, lora_ref, bones_ref, o_ref,
                 carry_ref, state_ref, r_s, k_s, v_s, kk_s, b_s, ld_s, y_s, *, exact):
    tt = pb_ref.shape[1]
    c_len = RW_C
    t_idx = pl.program_id(1)

    @pl.when(t_idx == 0)
    def _():
        carry_ref[...] = jnp.zeros_like(carry_ref)
        state_ref[...] = jnp.zeros_like(state_ref)

    p = pb_ref[0].astype(F32)
    prev = pltpu.roll(p, 1, axis=0)
    first = lax.broadcasted_iota(I32, p.shape, 0) == 0
    prev = jnp.where(first, jnp.broadcast_to(carry_ref[7:8, :], p.shape), prev)
    carry_ref[...] = p[tt - 8:, :]
    p = p + (prev - p) * mu_ref[...]

    w0, a0, k_k, k_a = vec_ref[0:1, :], vec_ref[1:2, :], vec_ref[2:3, :], vec_ref[3:4, :]
    r_k, ln_g, ln_b = vec_ref[4:5, :], vec_ref[5:6, :], vec_ref[6:7, :]
    bones = bones_ref[...]

    r = p[:, 0:512]
    k = p[:, 512:1024]
    v = p[:, 1024:1536]
    lo = p[:, 1536:1792]
    wl = _dot(jnp.tanh(lo), lora_ref[0])
    al = _dot(lo, lora_ref[1])
    g = _dot(jax.nn.sigmoid(lo), lora_ref[2])
    z = -(w0 + wl)
    w = -(jnp.maximum(z, 0.0) + jnp.log(1.0 + jnp.exp(-jnp.abs(z)))) - 0.5
    a = jax.nn.sigmoid(a0 + al)
    kk = k * k_k
    ssq = _dot_split(kk * kk, bones)
    kk = kk / jnp.maximum(jnp.sqrt(ssq), 1e-12)
    k = k * (1.0 + (a - 1.0) * k_a)
    r_s[...] = r
    k_s[...] = k
    v_s[...] = v
    kk_s[...] = kk
    b_s[...] = kk * a
    ld_s[...] = -jnp.exp(w)

    ii = lax.broadcasted_iota(I32, (c_len, c_len), 0)
    jj = lax.broadcasted_iota(I32, (c_len, c_len), 1)
    strict, incl = ii > jj, ii >= jj
    eye = jnp.where(ii == jj, 1.0, 0.0)
    tri = jnp.where(incl, 1.0, 0.0).astype(BF16)
    blk_masks = {1 << sh: (ii >> sh) == (jj >> sh) for sh in (3, 4, 5, 6)}

    dot = functools.partial(_dot, exact=exact)
    off_masks = [blk_masks[sz] & jnp.logical_not(blk_masks[sz // 2]) for sz in (16, 32, 64)]

    def chunk_body(ci, carry):
        probs = []
        for u in range(RW_PAR):
            rows = pl.ds(pl.multiple_of((ci * RW_PAR + u) * c_len, c_len), c_len)
            ld = ld_s[rows, :]
            cum = _dot_split(ld, tri, left=True)
            cum_end = cum[c_len - 1:c_len, :]
            e_inv, e_end = jnp.exp(-cum), jnp.exp(cum_end - cum)
            kc_, bc_ = k_s[rows, :], b_s[rows, :]
            kkt = kk_s[rows, :] * jnp.exp(cum - ld)
            rt = r_s[rows, :] * jnp.exp(cum)
            kd, bd = kc_ * e_inv, bc_ * e_inv
            kh, bh = kc_ * e_end, bc_ * e_end
            vc = v_s[rows, :]
            e_last = jnp.exp(cum_end)
            for h in range(RW_H):
                sl = slice(h * RW_N, (h + 1) * RW_N)
                probs.append(dict(rows=rows, sl=sl, h=h, kkt=kkt[:, sl], rt=rt[:, sl], kd=kd[:, sl], bd=bd[:, sl],
                                  kh=kh[:, sl], bh=bh[:, sl], v=vc[:, sl], e_last=e_last[:, sl]))
        for p in probs:
            big = dot(jnp.concatenate([p["kkt"], p["rt"]], axis=0),
                      jnp.concatenate([p["bd"], p["kd"]], axis=0), _NT)
            p["m_b"] = jnp.where(strict, big[:c_len, :c_len], 0.0)
            p["m_k"] = jnp.where(strict, big[:c_len, c_len:], 0.0)
            p["p_b"] = jnp.where(incl, big[c_len:, :c_len], 0.0)
            p["p_k"] = jnp.where(incl, big[c_len:, c_len:], 0.0)
            p["n8"] = jnp.where(blk_masks[8], p["m_b"], 0.0)
        for p in probs:
            p["n2"] = dot(p["n8"], p["n8"])
            p["mkv"] = dot(p["m_k"], p["v"])
        for p in probs:
            p["n4"] = dot(p["n2"], p["n2"])
            p["t"] = dot(eye - p["n8"], eye + p["n2"])
        for p in probs:
            p["t"] = dot(p["t"], eye + p["n4"])
        for om in off_masks:
            for p in probs:
                p["a"] = dot(p["t"], jnp.where(om, p["m_b"], 0.0))
            for p in probs:
                p["t"] = p["t"] - dot(p["a"], p["t"])
        for p in probs:
            p["wu"] = dot(p["t"], jnp.concatenate([p["kkt"], p["mkv"]], axis=1))
            p["pkv"] = dot(p["p_k"], p["v"])
        for p in probs:
            pwu = dot(p["p_b"], p["wu"])
            p["y0"] = p["pkv"] - pwu[:, RW_N:]
            p["gm"] = p["rt"] - pwu[:, :RW_N]
            p["bw"] = dot(p["bh"], p["wu"][:, :RW_N], _TN)
            p["psi_t"] = dot(jnp.concatenate([p["v"], p["wu"][:, RW_N:]], axis=0),
                             jnp.concatenate([p["kh"], -p["bh"]], axis=0), _TN)
        for p in probs:
            st = state_ref[p["h"]]
            y_s[p["rows"], p["sl"]] = p["y0"] + _dot(p["gm"], st, _NT, exact=True)
            state_ref[p["h"]] = st * p["e_last"] - _dot(st, p["bw"], _NT, exact=True) + p["psi_t"]
        return carry

    lax.fori_loop(0, tt // (c_len * RW_PAR), chunk_body, 0)

    y = y_s[...]
    r, k, v = r_s[...], k_s[...], v_s[...]
    mean = _dot_split(y, bones) * (1.0 / RW_N)
    yc = y - mean
    var = _dot_split(yc * yc, bones) * (1.0 / RW_N)
    y = yc * lax.rsqrt(var + RW_GN_EPS) * ln_g + ln_b
    y = y + _dot_split(r * k * r_k, bones) * v
    o_ref[0] = (y * g).astype(o_ref.dtype)


def rwkv_mix(pb, mu, vecs, lora, b, s, exact=False):
    tt = min(RW_TT, s)
    bones = (jnp.arange(BRANCH)[:, None] // RW_N == jnp.arange(BRANCH)[None, :] // RW_N).astype(BF16)
    const = lambda shape: pl.BlockSpec(shape, lambda bi, ti: (0,) * len(shape))
    sq = lambda: pltpu.VMEM((tt, BRANCH), F32)
    out = pl.pallas_call(
        functools.partial(_rwkv_kernel, exact=exact),
        grid=(b, s // tt),
        in_specs=[pl.BlockSpec((1, tt, PB_W), lambda bi, ti: (bi, ti, 0)),
                  const((1, PB_W)), const((8, BRANCH)), const((3, 256, BRANCH)), const((BRANCH, BRANCH))],
        out_specs=pl.BlockSpec((1, tt, BRANCH), lambda bi, ti: (bi, ti, 0)),
        out_shape=jax.ShapeDtypeStruct((b, s, BRANCH), BF16),
        scratch_shapes=[pltpu.VMEM((8, PB_W), F32), pltpu.VMEM((RW_H, RW_N, RW_N), F32),
                        sq(), sq(), sq(), sq(), sq(), sq(), sq()],
        compiler_params=_cp(("parallel", "arbitrary")),
        name="rwkv_mix",
    )(pb.reshape(b, s, PB_W), mu, vecs, lora, bones)
    return out.reshape(b * s, BRANCH)


def _pool_kernel(pc_ref, w_ref, sc_ref, o_ref, carry_ref):
    ts = pc_ref.shape[1]
    t_idx = pl.program_id(1)

    @pl.when(t_idx == 0)
    def _():
        carry_ref[...] = jnp.zeros_like(carry_ref)

    x = pc_ref[0].astype(F32)
    ext = jnp.concatenate([carry_ref[...], x], axis=0)
    carry_ref[...] = x[ts - 16:, :]
    n = ts + 16
    s2 = ext + pltpu.roll(ext, 1, axis=0)
    s4 = s2[:, 128:] + pltpu.roll(s2[:, 128:], 2, axis=0)
    s8 = s4[:, 128:] + pltpu.roll(s4[:, 128:], 4, axis=0)
    s16 = s8[:, 128:] + pltpu.roll(s8[:, 128:], 8, axis=0)
    sums = (s2[16:, 0:128], s4[16:, 0:128], s8[16:, 0:128], s16[16:, 0:128])
    step = (lax.broadcasted_iota(I32, (ts, POOL_GD), 0) + t_idx * ts + 1).astype(F32)
    outs = []
    for gi, win in enumerate(POOL_WINDOWS):
        mean = sums[gi] / jnp.minimum(step, float(win))
        pooled = mean - x[:, gi * POOL_GD:(gi + 1) * POOL_GD]
        outs.append(jnp.dot(pooled.astype(BF16), w_ref[gi], preferred_element_type=F32))
    del n
    o_ref[0] = (jnp.concatenate(outs, axis=1) * sc_ref[...]).astype(o_ref.dtype)


def pool_mix(pc, pool_w, pool_scale, b, s):
    ts = min(POOL_TS, s)
    out = pl.pallas_call(
        _pool_kernel,
        grid=(b, s // ts),
        in_specs=[pl.BlockSpec((1, ts, BRANCH), lambda bi, ti: (bi, ti, 0)),
                  pl.BlockSpec((4, POOL_GD, POOL_GD), lambda bi, ti: (0, 0, 0)),
                  pl.BlockSpec((1, BRANCH), lambda bi, ti: (0, 0))],
        out_specs=pl.BlockSpec((1, ts, BRANCH), lambda bi, ti: (bi, ti, 0)),
        out_shape=jax.ShapeDtypeStruct((b, s, BRANCH), BF16),
        scratch_shapes=[pltpu.VMEM((16, BRANCH), F32)],
        compiler_params=_cp(("parallel", "arbitrary")),
        name="pool_mix",
    )(pc.reshape(b, s, BRANCH), pool_w, pool_scale)
    return out.reshape(b * s, BRANCH)


def _ret_kernel(pd_ref, cos_ref, sin_ref, gn_ref, bones_ref, o_ref, state_ref, o_s):
    c_len = pd_ref.shape[1]
    t_idx = pl.program_id(1)

    @pl.when(t_idx == 0)
    def _():
        state_ref[...] = jnp.zeros_like(state_ref)

    cos, sin = cos_ref[0], sin_ref[0]
    q = _rope_apply(pd_ref[0, :, 0:512].astype(F32), cos, sin, RET_DK)
    k = _rope_apply(pd_ref[0, :, 512:1024].astype(F32), cos, sin, RET_DK) * (RET_DK ** -0.5)
    v = pd_ref[0, :, 1024:1536]
    gate = pd_ref[0, :, 1536:2048].astype(F32)

    ii = lax.broadcasted_iota(I32, (c_len, c_len), 0)
    jj = lax.broadcasted_iota(I32, (c_len, c_len), 1)
    diff = (ii - jj).astype(F32)
    pos = lax.broadcasted_iota(I32, (c_len, 1), 0).astype(F32)
    for h in range(RET_H):
        log_gamma = math.log(1.0 - 2.0 ** (-5.0 - h))
        intra = jnp.where(diff >= 0, jnp.exp(jnp.maximum(diff, 0.0) * log_gamma), 0.0)
        q_decay = jnp.exp((pos + 1.0) * log_gamma)
        k_decay = jnp.exp((c_len - 1.0 - pos) * log_gamma)
        c_decay = math.exp(c_len * log_gamma)
        sl = slice(h * 64, (h + 1) * 64)
        qh, kh, vh = q[:, sl], k[:, sl], v[:, sl]
        st = state_ref[h]
        sc = _dot(qh, kh, _NT) * intra
        o = _dot(sc, vh) + _dot(qh, st) * q_decay
        state_ref[h] = st * c_decay + _dot(kh * k_decay, vh, _TN)
        o_s[:, sl] = o

    o = o_s[...]
    bones = bones_ref[...]
    mean = _dot_split(o, bones) * (1.0 / RET_DV)
    oc = o - mean
    var = _dot_split(oc * oc, bones) * (1.0 / RET_DV)
    o = oc * lax.rsqrt(var + RET_GN_EPS) * gn_ref[...]
    o_ref[0] = (jax.nn.silu(gate) * o).astype(o_ref.dtype)


def ret_mix(pd, cos_r, sin_r, gn_g, b, s):
    c_len = min(RET_C, s)
    bones = (jnp.arange(BRANCH)[:, None] // RET_DV == jnp.arange(BRANCH)[None, :] // RET_DV).astype(BF16)
    out = pl.pallas_call(
        _ret_kernel,
        grid=(b, s // c_len),
        in_specs=[pl.BlockSpec((1, c_len, PD_W), lambda bi, ti: (bi, ti, 0)),
                  pl.BlockSpec((1, c_len, LANES), lambda bi, ti: (bi, ti, 0)),
                  pl.BlockSpec((1, c_len, LANES), lambda bi, ti: (bi, ti, 0)),
                  pl.BlockSpec((1, BRANCH), lambda bi, ti: (0, 0)),
                  pl.BlockSpec((BRANCH, BRANCH), lambda bi, ti: (0, 0))],
        out_specs=pl.BlockSpec((1, c_len, BRANCH), lambda bi, ti: (bi, ti, 0)),
        out_shape=jax.ShapeDtypeStruct((b, s, BRANCH), BF16),
        scratch_shapes=[pltpu.VMEM((RET_H, 64, RET_DV), F32), pltpu.VMEM((c_len, BRANCH), F32)],
        compiler_params=_cp(("parallel", "arbitrary")),
        name="ret_mix",
    )(pd.reshape(b, s, PD_W), cos_r.reshape(b, s, LANES), sin_r.reshape(b, s, LANES), gn_g, bones)
    return out.reshape(b * s, BRANCH)


def _merge_kernel(x_ref, pg_ref, oa_ref, ob_ref, oc_ref, od_ref, gb_ref, wb_ref, wo_ref, o_ref):
    merged = None
    for i, br in enumerate((oa_ref, ob_ref, oc_ref, od_ref)):
        gate = jax.nn.sigmoid(pg_ref[:, i * D_MODEL:(i + 1) * D_MODEL].astype(F32) + gb_ref[i:i + 1, :])
        term = gate * jnp.dot(br[...], wb_ref[i], preferred_element_type=F32)
        merged = term if merged is None else merged + term
    o_ref[...] = x_ref[...] + jnp.dot(merged.astype(BF16), wo_ref[...], preferred_element_type=F32)


def merge_out(x, pg, o_a, o_b, o_c, o_d, gate_b, w_branch, w_out):
    t = x.shape[0]
    tm = min(MERGE_TM, t)
    row = lambda w: pl.BlockSpec((tm, w), lambda i: (i, 0))
    return pl.pallas_call(
        _merge_kernel,
        grid=(t // tm,),
        in_specs=[row(D_MODEL), row(4 * D_MODEL), row(BRANCH), row(BRANCH), row(BRANCH), row(BRANCH),
                  pl.BlockSpec((4, D_MODEL), lambda i: (0, 0)),
                  pl.BlockSpec((4, BRANCH, D_MODEL), lambda i: (0, 0, 0)),
                  pl.BlockSpec((D_MODEL, D_MODEL), lambda i: (0, 0))],
        out_specs=row(D_MODEL),
        out_shape=jax.ShapeDtypeStruct((t, D_MODEL), F32),
        compiler_params=_cp(("parallel",)),
        name="merge_out",
    )(x, pg, o_a, o_b, o_c, o_d, gate_b, w_branch, w_out)


def _mlp_kernel(x_ref, g_ref, up_ref, down_ref, fg_ref, o_ref, h_ref, acc_ref, *, final_norm):
    j = pl.program_id(1)

    @pl.when(j == 0)
    def _():
        x = x_ref[...]
        ms = jnp.mean(x * x, axis=-1, keepdims=True)
        h_ref[...] = (x * lax.rsqrt(ms + NORM_EPS) * g_ref[...]).astype(BF16)
        acc_ref[...] = x

    u = jnp.dot(h_ref[...], up_ref[...], preferred_element_type=F32)
    u = jnp.square(jnp.maximum(u, 0.0))
    acc_ref[...] += jnp.dot(u.astype(BF16), down_ref[...], preferred_element_type=F32)

    @pl.when(j == pl.num_programs(1) - 1)
    def _():
        y = acc_ref[...]
        if final_norm:
            ms = jnp.mean(y * y, axis=-1, keepdims=True)
            y = y * lax.rsqrt(ms + NORM_EPS) * fg_ref[...]
        o_ref[...] = y


def mlp_block(x, g, up, down, final_g, final_norm):
    t = x.shape[0]
    tm = min(MLP_TM, t)
    return pl.pallas_call(
        functools.partial(_mlp_kernel, final_norm=final_norm),
        grid=(t // tm, D_FF // MLP_FF),
        in_specs=[pl.BlockSpec((tm, D_MODEL), lambda i, j: (i, 0)),
                  pl.BlockSpec((1, D_MODEL), lambda i, j: (0, 0)),
                  pl.BlockSpec((D_MODEL, MLP_FF), lambda i, j: (0, j)),
                  pl.BlockSpec((MLP_FF, D_MODEL), lambda i, j: (j, 0)),
                  pl.BlockSpec((1, D_MODEL), lambda i, j: (0, 0))],
        out_specs=pl.BlockSpec((tm, D_MODEL), lambda i, j: (i, 0)),
        out_shape=jax.ShapeDtypeStruct((t, D_MODEL), F32),
        scratch_shapes=[pltpu.VMEM((tm, D_MODEL), BF16), pltpu.VMEM((tm, D_MODEL), F32)],
        compiler_params=_cp(("parallel", "arbitrary")),
        name="mlp_block",
    )(x, g, up, down, final_g)


def _partner_cols(w, heads, dh, rot):
    lead = w.shape[:-1]
    w3 = w.reshape(*lead, heads, dh)
    half = rot // 2
    part = jnp.concatenate([-w3[..., half:rot], w3[..., :half], jnp.zeros_like(w3[..., rot:])], axis=-1)
    return w3, part


def _interleave(w, heads, dh, rot):
    w3, part = _partner_cols(w, heads, dh, rot)
    return jnp.concatenate([w3, part], axis=-1).reshape(*w.shape[:-1], heads * 2 * dh)


def _prep_weights(w_in, rwkv_mu):
    n_a = 512 + 128 + 128 + 256 + 64 + 4
    n_b = 1696
    off_b, off_c, off_d = n_a, n_a + n_b, n_a + n_b + 512
    off_g = off_d + 1536
    wa = w_in[..., :n_a]
    q, k, v = wa[..., 0:512], wa[..., 512:640], wa[..., 640:768]
    qi, ki, wi = wa[..., 768:1024], wa[..., 1024:1088], wa[..., 1088:1092]
    pad = lambda w, n: jnp.pad(w, [(0, 0)] * (w.ndim - 1) + [(0, n - w.shape[-1])])
    w_a = jnp.concatenate([_interleave(q, ATT_HEADS, ATT_DH, ROT_DIM), _interleave(qi, IDX_HEADS, IDX_DH, ROT_DIM),
                           _interleave(k, ATT_KV, ATT_DH, ROT_DIM), _interleave(ki, 1, IDX_DH, ROT_DIM),
                           v, pad(wi, 128)], axis=-1)
    w_a = pad(w_a, PA_W)
    wb = w_in[..., off_b:off_c]
    perm = lambda z: jnp.concatenate([z[..., 0:512], z[..., 544:1056], z[..., 1056:1568],
                                      z[..., 512:544], z[..., 1568:1600], z[..., 1600:1696]], axis=-1)
    w_b = pad(perm(wb), PB_W)
    mu = pad(perm(rwkv_mu), PB_W)
    w_c = w_in[..., off_c:off_d]
    wd = w_in[..., off_d:off_g]
    rq, rk, rv, rg = wd[..., 0:256], wd[..., 256:512], wd[..., 512:1024], wd[..., 1024:1536]
    w_d = jnp.concatenate([_interleave(rq, RET_H, RET_DK, RET_DK), _interleave(rk, RET_H, RET_DK, RET_DK), rv, rg],
                          axis=-1)
    w_g = w_in[..., off_g:]
    cast = lambda w: w.astype(BF16)
    return cast(w_a), cast(w_b), cast(w_c), cast(w_d), cast(w_g), mu


def kernel(x, positions, attn_norm_g, w_in, rwkv_mu, rwkv_w0, rwkv_w2, rwkv_a0, rwkv_a2, rwkv_g2,
           rwkv_k_k, rwkv_k_a, rwkv_r_k, rwkv_ln_g, rwkv_ln_b, pool_w, pool_scale, ret_gn_g,
           gate_b, w_branch, w_out, mlp_norm_g, mlp_up, mlp_down, final_norm_g):
    b, s, d = x.shape
    depth = w_in.shape[0]
    t = b * s
    w_a, w_b, w_c, w_d, w_g, mu = _prep_weights(w_in, rwkv_mu)
    zeros = jnp.zeros_like(rwkv_w0)
    vecs = jnp.stack([rwkv_w0, rwkv_a0, rwkv_k_k, rwkv_k_a, rwkv_r_k, rwkv_ln_g, rwkv_ln_b, zeros], axis=1)
    lora = jnp.zeros((depth, 3, 256, BRANCH), F32)
    lora = lora.at[:, 0, 0:RW_DL].set(rwkv_w2)
    lora = lora.at[:, 1, RW_DL:RW_DL + RW_AL].set(rwkv_a2)
    lora = lora.at[:, 2, RW_DL + RW_AL:RW_DL + RW_AL + RW_GL].set(rwkv_g2)
    lora = lora.astype(BF16)
    pool_w16, w_branch16, w_out16 = pool_w.astype(BF16), w_branch.astype(BF16), w_out.astype(BF16)
    up16, down16 = mlp_up.astype(BF16), mlp_down.astype(BF16)

    cos_a, sin_a, cos_r, sin_r = rope_tables(positions)
    xf = x.reshape(t, d)
    for l in range(depth):
        g = attn_norm_g[l][None]
        pa = norm_matmul(xf, g, w_a[l], F32)
        pb = norm_matmul(xf, g, w_b[l], BF16)
        pc = norm_matmul(xf, g, w_c[l], BF16)
        pd = norm_matmul(xf, g, w_d[l], BF16)
        pg = norm_matmul(xf, g, w_g[l], BF16)
        q, qi, k, ki, vt, wi = dsa_prep(pa, cos_a, sin_a, b, s)
        o_a = dsa_attention(q, qi, wi, k, ki, vt, b, s)
        o_b = rwkv_mix(pb, mu[l][None], vecs[l], lora[l], b, s)
        o_c = pool_mix(pc, pool_w16[l], pool_scale[l][None], b, s)
        o_d = ret_mix(pd, cos_r, sin_r, ret_gn_g[l][None], b, s)
        xf = merge_out(xf, pg, o_a, o_b, o_c, o_d, gate_b[l], w_branch16[l], w_out16[l])
        xf = mlp_block(xf, mlp_norm_g[l][None], up16[l], down16[l], final_norm_g[None], l == depth - 1)
    return xf.reshape(b, s, d)
```

```python
import functools
import math

import jax
import jax.numpy as jnp
from jax import lax
from jax.experimental import pallas as pl
from jax.experimental.pallas import tpu as pltpu

F32 = jnp.float32
BF16 = jnp.bfloat16
I32 = jnp.int32

D_MODEL = 1024
BRANCH = 512
ATT_HEADS, ATT_KV, ATT_DH, ROT_DIM = 8, 2, 64, 16
ROPE_THETA = 500000.0
IDX_HEADS, IDX_DH, INDEX_TOPK, Q_BLOCK = 4, 64, 256, 128
RW_N, RW_H = 64, 8
RW_DL, RW_AL, RW_GL = 32, 32, 96
RW_GN_EPS = 64e-5
POOL_WINDOWS = (2, 4, 8, 16)
POOL_GD = 128
RET_H, RET_DK, RET_DV = 8, 32, 64
RET_THETA = 10000.0
RET_GN_EPS = 1e-6
D_FF = 4096
NORM_EPS = 1e-5

LANES = 128
VMEM_LIMIT = 56 * 1024 * 1024

PROJ_TM = 1024
DSA_KC = 512
RW_C = 64
RW_TT = 512
RW_PAR = 4
POOL_TS = 512
RET_C = 256
MERGE_TM = 512
MLP_TM = 1024
MLP_FF = 1024

INT_MIN = -2 ** 31
SUBLANES = 8
GROUP_KEYS = 32 * SUBLANES
LOG2E = 1.4426950408889634
NEG_BIG = -1e30

PA_W = 2304
PA_Q, PA_QI, PA_K, PA_KI, PA_V, PA_WI = 0, 1024, 1536, 1792, 1920, 2048
PB_W = 1792
PD_W = 2048


def _cp(sem, vmem=VMEM_LIMIT):
    return pltpu.CompilerParams(dimension_semantics=sem, vmem_limit_bytes=vmem)


def _dot(a, b, dims=(((1,), (0,)), ((), ())), exact=False):
    if exact:
        return lax.dot_general(a.astype(F32), b.astype(F32), dims, precision=lax.Precision.HIGHEST,
                               preferred_element_type=F32)
    return lax.dot_general(a.astype(BF16), b.astype(BF16), dims, preferred_element_type=F32)


_NN = (((1,), (0,)), ((), ()))
_NT = (((1,), (1,)), ((), ()))
_TN = (((0,), (0,)), ((), ()))


def _dot_split(x, ones_rhs, dims=_NN, left=False):
    hi = x.astype(BF16)
    lo = (x - hi.astype(F32)).astype(BF16)
    if left:
        return (lax.dot_general(ones_rhs, hi, dims, preferred_element_type=F32)
                + lax.dot_general(ones_rhs, lo, dims, preferred_element_type=F32))
    return (lax.dot_general(hi, ones_rhs, dims, preferred_element_type=F32)
            + lax.dot_general(lo, ones_rhs, dims, preferred_element_type=F32))


def _norm_matmul_kernel(x_ref, g_ref, w_ref, o_ref, h_ref):
    @pl.when(pl.program_id(1) == 0)
    def _():
        x = x_ref[...]
        ms = jnp.mean(x * x, axis=-1, keepdims=True)
        h_ref[...] = (x * lax.rsqrt(ms + NORM_EPS) * g_ref[...]).astype(BF16)

    o_ref[...] = jnp.dot(h_ref[...], w_ref[...], preferred_element_type=F32).astype(o_ref.dtype)


def _pick_tn(n):
    for tn in (1024, 896, 768, 640, 512, 384, 256, 128):
        if n % tn == 0:
            return tn
    raise ValueError(n)


def norm_matmul(x, g, w, out_dtype):
    t, d = x.shape
    n = w.shape[1]
    tm, tn = min(PROJ_TM, t), _pick_tn(n)
    return pl.pallas_call(
        _norm_matmul_kernel,
        grid=(t // tm, n // tn),
        in_specs=[pl.BlockSpec((tm, d), lambda i, j: (i, 0)),
                  pl.BlockSpec((1, d), lambda i, j: (0, 0)),
                  pl.BlockSpec((d, tn), lambda i, j: (0, j))],
        out_specs=pl.BlockSpec((tm, tn), lambda i, j: (i, j)),
        out_shape=jax.ShapeDtypeStruct((t, n), out_dtype),
        scratch_shapes=[pltpu.VMEM((tm, d), BF16)],
        compiler_params=_cp(("parallel", "arbitrary")),
        name="norm_matmul",
    )(x, g, w)


def _rope_table_kernel(pos_ref, fa_ref, fr_ref, ma_ref, mr_ref, ca_ref, sa_ref, cr_ref, sr_ref):
    pos = pos_ref[...].astype(F32)
    ang_a = pos * fa_ref[...]
    ca_ref[...] = jnp.cos(ang_a) * ma_ref[...]
    sa_ref[...] = jnp.sin(ang_a) * ma_ref[...]
    ang_r = pos * fr_ref[...]
    cr_ref[...] = jnp.cos(ang_r) * mr_ref[...]
    sr_ref[...] = jnp.sin(ang_r) * mr_ref[...]


def rope_tables(positions):
    t = positions.size
    pos = positions.reshape(t, 1)
    inv_a = ROPE_THETA ** (-jnp.arange(0, ROT_DIM, 2, dtype=F32) / ROT_DIM)
    fa = jnp.concatenate([inv_a, inv_a, jnp.zeros((LANES - ROT_DIM,), F32)])[None]
    ma = jnp.concatenate([jnp.ones((ATT_DH,), F32), jnp.zeros((LANES - ATT_DH,), F32)])[None]
    inv_r = 1.0 / (RET_THETA ** jnp.linspace(0.0, 1.0, RET_DK // 2, dtype=F32))
    fr64 = jnp.concatenate([inv_r, inv_r, jnp.zeros((RET_DK,), F32)])
    fr = jnp.concatenate([fr64, fr64])[None]
    mr64 = jnp.concatenate([jnp.ones((RET_DK,), F32), jnp.zeros((RET_DK,), F32)])
    mr = jnp.concatenate([mr64, mr64])[None]
    ts = min(1024, t)
    row = pl.BlockSpec((1, LANES), lambda i: (0, 0))
    tab = pl.BlockSpec((ts, LANES), lambda i: (i, 0))
    return pl.pallas_call(
        _rope_table_kernel,
        grid=(t // ts,),
        in_specs=[pl.BlockSpec((ts, 1), lambda i: (i, 0)), row, row, row, row],
        out_specs=[tab, tab, tab, tab],
        out_shape=[jax.ShapeDtypeStruct((t, LANES), F32)] * 4,
        compiler_params=_cp(("parallel",)),
        name="rope_tables",
    )(pos, fa, fr, ma, mr)


def _rope_apply(p, cos, sin, half_block):
    w = p.shape[-1]
    reps = w // cos.shape[-1]
    c = jnp.concatenate([cos] * reps, axis=-1) if reps > 1 else cos
    s = jnp.concatenate([sin] * reps, axis=-1) if reps > 1 else sin
    partner = pltpu.roll(p, w - half_block, axis=1)
    return p * c + partner * s


def _dsa_prep_kernel(pa_ref, cos_ref, sin_ref, q_ref, qi_ref, k_ref, ki_ref, vt_ref, wi_ref):
    cos, sin = cos_ref[0], sin_ref[0]
    q_scale = ATT_DH ** -0.5 * LOG2E
    q_ref[0] = (_rope_apply(pa_ref[0, :, PA_Q:PA_Q + 1024], cos, sin, ATT_DH) * q_scale).astype(BF16)
    qi_ref[0] = _rope_apply(pa_ref[0, :, PA_QI:PA_QI + 512], cos, sin, ATT_DH).astype(BF16)
    k_ref[0] = _rope_apply(pa_ref[0, :, PA_K:PA_K + 256], cos, sin, ATT_DH).astype(BF16)
    ki_ref[0] = _rope_apply(pa_ref[0, :, PA_KI:PA_KI + 128], cos, sin, ATT_DH).astype(BF16)
    vt = pa_ref[0, :, PA_V:PA_V + 128].T
    ones = jnp.ones((ATT_DH, vt.shape[1]), F32)
    vt_ref[0] = jnp.concatenate([vt[:ATT_DH], ones, vt[ATT_DH:], ones], axis=0).astype(BF16)
    wi_ref[0] = pa_ref[0, :, PA_WI:PA_WI + 128]


def dsa_prep(pa, cos_a, sin_a, b, s):
    tq = min(512, s)
    blk = lambda w: pl.BlockSpec((1, tq, w), lambda bi, i: (bi, i, 0))
    r3 = lambda a: a.reshape(b, s, a.shape[-1])
    shp = lambda w, dt: jax.ShapeDtypeStruct((b, s, w), dt)
    return pl.pallas_call(
        _dsa_prep_kernel,
        grid=(b, s // tq),
        in_specs=[blk(PA_W), blk(LANES), blk(LANES)],
        out_specs=[blk(1024), blk(512), blk(256), blk(128),
                   pl.BlockSpec((1, 256, tq), lambda bi, i: (bi, 0, i)), blk(128)],
        out_shape=[shp(1024, BF16), shp(512, BF16), shp(256, BF16), shp(128, BF16),
                   jax.ShapeDtypeStruct((b, 256, s), BF16), shp(128, F32)],
        compiler_params=_cp(("parallel", "parallel")),
        name="dsa_prep",
    )(r3(pa), r3(cos_a), r3(sin_a))


def _bit_transpose32(words):
    a = list(words)
    for sh, mask in ((16, 0x0000FFFF), (8, 0x00FF00FF), (4, 0x0F0F0F0F), (2, 0x33333333), (1, 0x55555555)):
        for k in range(32):
            if k & sh:
                continue
            t = (jnp.right_shift(a[k], sh) ^ a[k + sh]) & mask
            a[k + sh] = a[k + sh] ^ t
            a[k] = a[k] ^ jnp.left_shift(t, sh)
    return a


def _dsa_kernel(q_ref, qi_ref, wi_ref, k_ref, ki_ref, vt_ref, o_ref, keys_ref, planes_ref, live_ref, *, kc, topk):
    qb = Q_BLOCK
    i = pl.program_id(1)
    n_kc = (i * qb + qb + kc - 1) // kc
    rep = ATT_HEADS // ATT_KV

    @pl.when(i == 0)
    def _():
        keys_ref[...] = jnp.full(keys_ref.shape, INT_MIN, I32)
        planes_ref[...] = jnp.zeros(planes_ref.shape, I32)

    qi = qi_ref[0]
    qis = jnp.concatenate([qi[:, h * LANES:(h + 1) * LANES] for h in range(IDX_HEADS)], axis=0)
    wit = wi_ref[0].T
    w_all = jnp.concatenate([wit[h:h + 1, :] for h in range(IDX_HEADS)], axis=1)
    kpos = lax.broadcasted_iota(I32, (kc, qb), 0)
    qpos = lax.broadcasted_iota(I32, (kc, qb), 1) + i * qb

    def score_body(c, carry):
        kic = ki_ref[0, pl.ds(pl.multiple_of(c * kc, kc), kc), :]
        d = lax.dot_general(kic, qis, _NT, preferred_element_type=F32)
        d = jnp.maximum(d, 0.0) * w_all
        acc = d[:, 0:qb]
        for h in range(1, IDX_HEADS):
            acc = acc + d[:, h * qb:(h + 1) * qb]
        acc = jnp.where(acc == 0.0, 0.0, acc)
        bits = pltpu.bitcast(acc, I32)
        key = jnp.where(bits < 0, bits ^ 0x7FFFFFFF, bits)
        key = jnp.where(kpos + c * kc <= qpos, key, INT_MIN)
        keys_ref[c] = key
        u = key ^ INT_MIN
        for g in range(kc // GROUP_KEYS):
            words = _bit_transpose32([u[g * GROUP_KEYS + SUBLANES * j:g * GROUP_KEYS + SUBLANES * (j + 1), :]
                                      for j in range(32)])
            for p in range(32):
                planes_ref[p, c * (kc // GROUP_KEYS) + g] = words[p]
        return carry

    lax.fori_loop(0, n_kc, score_body, 0)

    n_groups = planes_ref.shape[1]
    n_live = n_kc * (kc // GROUP_KEYS)
    for g in range(n_groups):
        live_ref[g] = jnp.full((SUBLANES, qb), jnp.where(g < n_live, -1, 0), I32)
    kf = jnp.full((1, qb), float(topk), F32)

    def popcount_rows(words):
        cnt = lax.population_count(words)
        tot = cnt[0]
        for g in range(1, n_groups):
            tot = tot + cnt[g]
        return jnp.sum(tot.astype(F32), axis=0, keepdims=True)

    def bit_body(it, carry):
        n_gt, prefix = carry
        p = 31 - it
        plane = planes_ref[p]
        live = live_ref[...]
        n_one = popcount_rows(live & plane)
        take = (n_gt + n_one) >= kf
        live_ref[...] = live & (plane ^ jnp.where(take, 0, -1)[None])
        return jnp.where(take, n_gt, n_gt + n_one), prefix | jnp.where(take, jnp.left_shift(jnp.int32(1), p), 0)

    n_gt, prefix = lax.fori_loop(0, 32, bit_body, (jnp.zeros((1, qb), F32), jnp.zeros((1, qb), I32)))
    ans = prefix ^ INT_MIN
    need = kf - n_gt
    cnt_ans = n_gt + popcount_rows(live_ref[...])

    tie = (ans > INT_MIN) & (cnt_ans > kf)
    any_tie = jnp.max(jnp.where(tie, 1.0, 0.0)) > 0.0

    @pl.when(any_tie)
    def _():
        ansb = jnp.broadcast_to(ans, (kc, qb))
        tieb = jnp.broadcast_to(jnp.where(tie, 1.0, 0.0), (kc, qb)) > 0.0
        lower = (lax.broadcasted_iota(I32, (kc, kc), 0) >= lax.broadcasted_iota(I32, (kc, kc), 1))
        lower = jnp.where(lower, 1.0, 0.0).astype(BF16)

        def tie_body(c, run):
            blk = keys_ref[c]
            eq = (blk == ansb) & tieb
            eqf = jnp.where(eq, 1.0, 0.0).astype(BF16)
            pref = jnp.dot(lower, eqf, preferred_element_type=F32) + run
            keys_ref[c] = jnp.where(eq & (pref > need), INT_MIN, blk)
            return pref[kc - 1:kc, :]

        lax.fori_loop(0, n_kc, tie_body, jnp.zeros((1, qb), F32))

    thr = jnp.broadcast_to(jnp.maximum(ans, INT_MIN + 1), (kc, qb))

    q = q_ref[0]
    qg = [jnp.concatenate([q[:, (g * rep + r) * LANES:(g * rep + r + 1) * LANES] for r in range(rep)], axis=0)
          for g in range(ATT_KV)]

    def attend(first, n_chunks, carry):
        carry = list(carry)

        def prep(u):
            bias = jnp.where(keys_ref[first + u] >= thr, 0.0, NEG_BIG)
            keys_c = pl.ds(pl.multiple_of((first + u) * kc, kc), kc)
            return jnp.concatenate([bias] * rep, axis=1), keys_c

        def scores(g, chunk):
            bias, keys_c = chunk
            kg = k_ref[0, keys_c, g * LANES:(g + 1) * LANES]
            return lax.dot_general(kg, qg[g], _NT, preferred_element_type=F32) + bias

        def softmax_pv(g, st, chunk):
            m_old = carry[2 * g]
            m_new = jnp.maximum(m_old, jnp.max(st, axis=0, keepdims=True))
            pt = jnp.exp2(st - m_new).astype(BF16)
            pv = jnp.dot(vt_ref[0, g * LANES:(g + 1) * LANES, chunk[1]], pt, preferred_element_type=F32)
            carry[2 * g] = m_new
            carry[2 * g + 1] = carry[2 * g + 1] * jnp.exp2(m_old - m_new) + pv

        chunks = [prep(u) for u in range(n_chunks)]
        steps = [(g, ch) for ch in chunks for g in range(ATT_KV)]
        pending = [scores(*steps[0]), scores(*steps[1])]
        for n, (g, ch) in enumerate(steps):
            st = pending.pop(0)
            if n + 2 < len(steps):
                pending.append(scores(*steps[n + 2]))
            softmax_pv(g, st, ch)
        return tuple(carry)

    init = []
    for g in range(ATT_KV):
        init += [jnp.full((1, rep * qb), NEG_BIG, F32), jnp.zeros((LANES, rep * qb), F32)]
    res = lax.fori_loop(0, n_kc // 2, lambda j, carry: attend(2 * j, 2, carry), tuple(init))
    res = lax.cond(n_kc % 2 == 1, lambda carry: attend(n_kc - 1, 1, carry), lambda carry: carry, res)

    outs = []
    for g in range(ATT_KV):
        acc = res[2 * g + 1]
        ot = acc[:ATT_DH] / acc[ATT_DH:ATT_DH + 1]
        outs += [ot[:, r * qb:(r + 1) * qb] for r in range(rep)]
    o_ref[0] = jnp.concatenate(outs, axis=0).T.astype(o_ref.dtype)


def dsa_attention(q, qi, wi, k, ki, vt, b, s):
    kc = min(DSA_KC, s)
    topk = min(INDEX_TOPK, s // 4)
    qblk = lambda w: pl.BlockSpec((1, Q_BLOCK, w), lambda bi, i: (bi, i, 0))
    full = lambda w: pl.BlockSpec((1, s, w), lambda bi, i: (bi, 0, 0))
    out = pl.pallas_call(
        functools.partial(_dsa_kernel, kc=kc, topk=topk),
        grid=(b, s // Q_BLOCK),
        in_specs=[qblk(1024), qblk(512), qblk(128), full(256), full(128),
                  pl.BlockSpec((1, 256, s), lambda bi, i: (bi, 0, 0))],
        out_specs=qblk(BRANCH),
        out_shape=jax.ShapeDtypeStruct((b, s, BRANCH), BF16),
        scratch_shapes=[pltpu.VMEM((s // kc, kc, Q_BLOCK), I32),
                        pltpu.VMEM((32, s // GROUP_KEYS, SUBLANES, Q_BLOCK), I32),
                        pltpu.VMEM((s // GROUP_KEYS, SUBLANES, Q_BLOCK), I32)],
        compiler_params=_cp(("parallel", "arbitrary")),
        name="dsa_attention",
    )(q, qi, wi, k, ki, vt)
    return out.reshape(b * s, BRANCH)


def _rwkv_kernel(pb_ref, mu_ref, vec_ref, lora_ref, bones_ref, o_ref,
                 carry_ref, state_ref, r_s, k_s, v_s, kk_s, b_s, ld_s, y_s, *, exact):
    tt = pb_ref.shape[1]
    c_len = RW_C
    t_idx = pl.program_id(1)

    @pl.when(t_idx == 0)
    def _():
        carry_ref[...] = jnp.zeros_like(carry_ref)
        state_ref[...] = jnp.zeros_like(state_ref)

    p = pb_ref[0].astype(F32)
    prev = pltpu.roll(p, 1, axis=0)
    first = lax.broadcasted_iota(I32, p.shape, 0) == 0
    prev = jnp.where(first, jnp.broadcast_to(carry_ref[7:8, :], p.shape), prev)
    carry_ref[...] = p[tt - 8:, :]
    p = p + (prev - p) * mu_ref[...]

    w0, a0, k_k, k_a = vec_ref[0:1, :], vec_ref[1:2, :], vec_ref[2:3, :], vec_ref[3:4, :]
    r_k, ln_g, ln_b = vec_ref[4:5, :], vec_ref[5:6, :], vec_ref[6:7, :]
    bones = bones_ref[...]

    r = p[:, 0:512]
    k = p[:, 512:1024]
    v = p[:, 1024:1536]
    lo = p[:, 1536:1792]
    wl = _dot(jnp.tanh(lo), lora_ref[0])
    al = _dot(lo, lora_ref[1])
    g = _dot(jax.nn.sigmoid(lo), lora_ref[2])
    z = -(w0 + wl)
    w = -(jnp.maximum(z, 0.0) + jnp.log(1.0 + jnp.exp(-jnp.abs(z)))) - 0.5
    a = jax.nn.sigmoid(a0 + al)
    kk = k * k_k
    ssq = _dot_split(kk * kk, bones)
    kk = kk / jnp.maximum(jnp.sqrt(ssq), 1e-12)
    k = k * (1.0 + (a - 1.0) * k_a)
    r_s[...] = r
    k_s[...] = k
    v_s[...] = v
    kk_s[...] = kk
    b_s[...] = kk * a
    ld_s[...] = -jnp.exp(w)

    ii = lax.broadcasted_iota(I32, (c_len, c_len), 0)
    jj = lax.broadcasted_iota(I32, (c_len, c_len), 1)
    strict, incl = ii > jj, ii >= jj
    eye = jnp.where(ii == jj, 1.0, 0.0)
    tri = jnp.where(incl, 1.0, 0.0).astype(BF16)
    blk_masks = {1 << sh: (ii >> sh) == (jj >> sh) for sh in (3, 4, 5, 6)}

    dot = functools.partial(_dot, exact=exact)
    off_masks = [blk_masks[sz] & jnp.logical_not(blk_masks[sz // 2]) for sz in (16, 32, 64)]

    def chunk_body(ci, carry):
        probs = []
        for u in range(RW_PAR):
            rows = pl.ds(pl.multiple_of((ci * RW_PAR + u) * c_len, c_len), c_len)
            ld = ld_s[rows, :]
            cum = _dot_split(ld, tri, left=True)
            cum_end = cum[c_len - 1:c_len, :]
            e_inv, e_end = jnp.exp(-cum), jnp.exp(cum_end - cum)
            kc_, bc_ = k_s[rows, :], b_s[rows, :]
            kkt = kk_s[rows, :] * jnp.exp(cum - ld)
            rt = r_s[rows, :] * jnp.exp(cum)
            kd, bd = kc_ * e_inv, bc_ * e_inv
            kh, bh = kc_ * e_end, bc_ * e_end
            vc = v_s[rows, :]
            e_last = jnp.exp(cum_end)
            for h in range(RW_H):
                sl = slice(h * RW_N, (h + 1) * RW_N)
                probs.append(dict(rows=rows, sl=sl, h=h, kkt=kkt[:, sl], rt=rt[:, sl], kd=kd[:, sl], bd=bd[:, sl],
                                  kh=kh[:, sl], bh=bh[:, sl], v=vc[:, sl], e_last=e_last[:, sl]))
        for p in probs:
            big = dot(jnp.concatenate([p["kkt"], p["rt"]], axis=0),
                      jnp.concatenate([p["bd"], p["kd"]], axis=0), _NT)
            p["m_b"] = jnp.where(strict, big[:c_len, :c_len], 0.0)
            p["m_k"] = jnp.where(strict, big[:c_len, c_len:], 0.0)
            p["p_b"] = jnp.where(incl, big[c_len:, :c_len], 0.0)
            p["p_k"] = jnp.where(incl, big[c_len:, c_len:], 0.0)
            p["n8"] = jnp.where(blk_masks[8], p["m_b"], 0.0)
        for p in probs:
            p["n2"] = dot(p["n8"], p["n8"])
            p["mkv"] = dot(p["m_k"], p["v"])
        for p in probs:
            p["n4"] = dot(p["n2"], p["n2"])
            p["t"] = dot(eye - p["n8"], eye + p["n2"])
        for p in probs:
            p["t"] = dot(p["t"], eye + p["n4"])
        for om in off_masks:
            for p in probs:
                p["a"] = dot(p["t"], jnp.where(om, p["m_b"], 0.0))
            for p in probs:
                p["t"] = p["t"] - dot(p["a"], p["t"])
        for p in probs:
            p["wu"] = dot(p["t"], jnp.concatenate([p["kkt"], p["mkv"]], axis=1))
            p["pkv"] = dot(p["p_k"], p["v"])
        for p in probs:
            pwu = dot(p["p_b"], p["wu"])
            p["y0"] = p["pkv"] - pwu[:, RW_N:]
            p["gm"] = p["rt"] - pwu[:, :RW_N]
            p["bw"] = dot(p["bh"], p["wu"][:, :RW_N], _TN)
            p["psi_t"] = dot(jnp.concatenate([p["v"], p["wu"][:, RW_N:]], axis=0),
                             jnp.concatenate([p["kh"], -p["bh"]], axis=0), _TN)
        for p in probs:
            st = state_ref[p["h"]]
            y_s[p["rows"], p["sl"]] = p["y0"] + _dot(p["gm"], st, _NT, exact=True)
            state_ref[p["h"]] = st * p["e_last"] - _dot(st, p["bw"], _NT, exact=True) + p["psi_t"]
        return carry

    lax.fori_loop(0, tt // (c_len * RW_PAR), chunk_body, 0)

    y = y_s[...]
    r, k, v = r_s[...], k_s[...], v_s[...]
    mean = _dot_split(y, bones) * (1.0 / RW_N)
    yc = y - mean
    var = _dot_split(yc * yc, bones) * (1.0 / RW_N)
    y = yc * lax.rsqrt(var + RW_GN_EPS) * ln_g + ln_b
    y = y + _dot_split(r * k * r_k, bones) * v
    o_ref[0] = (y * g).astype(o_ref.dtype)


def rwkv_mix(pb, mu, vecs, lora, b, s, exact=False):
    tt = min(RW_TT, s)
    bones = (jnp.arange(BRANCH)[:, None] // RW_N == jnp.arange(BRANCH)[None, :] // RW_N).astype(BF16)
    const = lambda shape: pl.BlockSpec(shape, lambda bi, ti: (0,) * len(shape))
    sq = lambda: pltpu.VMEM((tt, BRANCH), F32)
    out = pl.pallas_call(
        functools.partial(_rwkv_kernel, exact=exact),
        grid=(b, s // tt),
        in_specs=[pl.BlockSpec((1, tt, PB_W), lambda bi, ti: (bi, ti, 0)),
                  const((1, PB_W)), const((8, BRANCH)), const((3, 256, BRANCH)), const((BRANCH, BRANCH))],
        out_specs=pl.BlockSpec((1, tt, BRANCH), lambda bi, ti: (bi, ti, 0)),
        out_shape=jax.ShapeDtypeStruct((b, s, BRANCH), BF16),
        scratch_shapes=[pltpu.VMEM((8, PB_W), F32), pltpu.VMEM((RW_H, RW_N, RW_N), F32),
                        sq(), sq(), sq(), sq(), sq(), sq(), sq()],
        compiler_params=_cp(("parallel", "arbitrary")),
        name="rwkv_mix",
    )(pb.reshape(b, s, PB_W), mu, vecs, lora, bones)
    return out.reshape(b * s, BRANCH)


def _pool_kernel(pc_ref, w_ref, sc_ref, o_ref, carry_ref):
    ts = pc_ref.shape[1]
    t_idx = pl.program_id(1)

    @pl.when(t_idx == 0)
    def _():
        carry_ref[...] = jnp.zeros_like(carry_ref)

    x = pc_ref[0].astype(F32)
    ext = jnp.concatenate([carry_ref[...], x], axis=0)
    carry_ref[...] = x[ts - 16:, :]
    n = ts + 16
    s2 = ext + pltpu.roll(ext, 1, axis=0)
    s4 = s2[:, 128:] + pltpu.roll(s2[:, 128:], 2, axis=0)
    s8 = s4[:, 128:] + pltpu.roll(s4[:, 128:], 4, axis=0)
    s16 = s8[:, 128:] + pltpu.roll(s8[:, 128:], 8, axis=0)
    sums = (s2[16:, 0:128], s4[16:, 0:128], s8[16:, 0:128], s16[16:, 0:128])
    step = (lax.broadcasted_iota(I32, (ts, POOL_GD), 0) + t_idx * ts + 1).astype(F32)
    outs = []
    for gi, win in enumerate(POOL_WINDOWS):
        mean = sums[gi] / jnp.minimum(step, float(win))
        pooled = mean - x[:, gi * POOL_GD:(gi + 1) * POOL_GD]
        outs.append(jnp.dot(pooled.astype(BF16), w_ref[gi], preferred_element_type=F32))
    del n
    o_ref[0] = (jnp.concatenate(outs, axis=1) * sc_ref[...]).astype(o_ref.dtype)


def pool_mix(pc, pool_w, pool_scale, b, s):
    ts = min(POOL_TS, s)
    out = pl.pallas_call(
        _pool_kernel,
        grid=(b, s // ts),
        in_specs=[pl.BlockSpec((1, ts, BRANCH), lambda bi, ti: (bi, ti, 0)),
                  pl.BlockSpec((4, POOL_GD, POOL_GD), lambda bi, ti: (0, 0, 0)),
                  pl.BlockSpec((1, BRANCH), lambda bi, ti: (0, 0))],
        out_specs=pl.BlockSpec((1, ts, BRANCH), lambda bi, ti: (bi, ti, 0)),
        out_shape=jax.ShapeDtypeStruct((b, s, BRANCH), BF16),
        scratch_shapes=[pltpu.VMEM((16, BRANCH), F32)],
        compiler_params=_cp(("parallel", "arbitrary")),
        name="pool_mix",
    )(pc.reshape(b, s, BRANCH), pool_w, pool_scale)
    return out.reshape(b * s, BRANCH)


def _ret_kernel(pd_ref, cos_ref, sin_ref, gn_ref, bones_ref, o_ref, state_ref, o_s, intra_ref):
    c_len = pd_ref.shape[1]
    t_idx = pl.program_id(1)

    log_gamma = [math.log(1.0 - 2.0 ** (-5.0 - h)) for h in range(RET_H)]

    @pl.when(t_idx == 0)
    def _():
        state_ref[...] = jnp.zeros_like(state_ref)
        ii = lax.broadcasted_iota(I32, (c_len, c_len), 0)
        jj = lax.broadcasted_iota(I32, (c_len, c_len), 1)
        diff = (ii - jj).astype(F32)
        for h in range(RET_H):
            intra_ref[h] = jnp.where(diff >= 0, jnp.exp(jnp.maximum(diff, 0.0) * log_gamma[h]), 0.0)

    cos, sin = cos_ref[0], sin_ref[0]
    q = _rope_apply(pd_ref[0, :, 0:512].astype(F32), cos, sin, RET_DK)
    k = _rope_apply(pd_ref[0, :, 512:1024].astype(F32), cos, sin, RET_DK) * (RET_DK ** -0.5)
    v = pd_ref[0, :, 1024:1536]
    gate = pd_ref[0, :, 1536:2048].astype(F32)

    pos = lax.broadcasted_iota(I32, (c_len, 1), 0).astype(F32)
    heads = [slice(h * 64, (h + 1) * 64) for h in range(RET_H)]
    scs = [_dot(q[:, sl], k[:, sl], _NT) for sl in heads]
    inter = [_dot(q[:, sl], state_ref[h]) for h, sl in enumerate(heads)]
    kv = [_dot(k[:, sl] * jnp.exp((c_len - 1.0 - pos) * log_gamma[h]), v[:, sl], _TN) for h, sl in enumerate(heads)]
    for h, sl in enumerate(heads):
        o = _dot(scs[h] * intra_ref[h], v[:, sl]) + inter[h] * jnp.exp((pos + 1.0) * log_gamma[h])
        state_ref[h] = state_ref[h] * math.exp(c_len * log_gamma[h]) + kv[h]
        o_s[:, sl] = o

    o = o_s[...]
    bones = bones_ref[...]
    mean = _dot_split(o, bones) * (1.0 / RET_DV)
    oc = o - mean
    var = _dot_split(oc * oc, bones) * (1.0 / RET_DV)
    o = oc * lax.rsqrt(var + RET_GN_EPS) * gn_ref[...]
    o_ref[0] = (jax.nn.silu(gate) * o).astype(o_ref.dtype)


def ret_mix(pd, cos_r, sin_r, gn_g, b, s):
    c_len = min(RET_C, s)
    bones = (jnp.arange(BRANCH)[:, None] // RET_DV == jnp.arange(BRANCH)[None, :] // RET_DV).astype(BF16)
    out = pl.pallas_call(
        _ret_kernel,
        grid=(b, s // c_len),
        in_specs=[pl.BlockSpec((1, c_len, PD_W), lambda bi, ti: (bi, ti, 0)),
                  pl.BlockSpec((1, c_len, LANES), lambda bi, ti: (bi, ti, 0)),
                  pl.BlockSpec((1, c_len, LANES), lambda bi, ti: (bi, ti, 0)),
                  pl.BlockSpec((1, BRANCH), lambda bi, ti: (0, 0)),
                  pl.BlockSpec((BRANCH, BRANCH), lambda bi, ti: (0, 0))],
        out_specs=pl.BlockSpec((1, c_len, BRANCH), lambda bi, ti: (bi, ti, 0)),
        out_shape=jax.ShapeDtypeStruct((b, s, BRANCH), BF16),
        scratch_shapes=[pltpu.VMEM((RET_H, 64, RET_DV), F32), pltpu.VMEM((c_len, BRANCH), F32),
                        pltpu.VMEM((RET_H, c_len, c_len), F32)],
        compiler_params=_cp(("parallel", "arbitrary")),
        name="ret_mix",
    )(pd.reshape(b, s, PD_W), cos_r.reshape(b, s, LANES), sin_r.reshape(b, s, LANES), gn_g, bones)
    return out.reshape(b * s, BRANCH)


def _merge_kernel(x_ref, pg_ref, oa_ref, ob_ref, oc_ref, od_ref, gb_ref, wb_ref, wo_ref, o_ref):
    merged = None
    for i, br in enumerate((oa_ref, ob_ref, oc_ref, od_ref)):
        gate = jax.nn.sigmoid(pg_ref[:, i * D_MODEL:(i + 1) * D_MODEL].astype(F32) + gb_ref[i:i + 1, :])
        term = gate * jnp.dot(br[...], wb_ref[i], preferred_element_type=F32)
        merged = term if merged is None else merged + term
    o_ref[...] = x_ref[...] + jnp.dot(merged.astype(BF16), wo_ref[...], preferred_element_type=F32)


def merge_out(x, pg, o_a, o_b, o_c, o_d, gate_b, w_branch, w_out):
    t = x.shape[0]
    tm = min(MERGE_TM, t)
    row = lambda w: pl.BlockSpec((tm, w), lambda i: (i, 0))
    return pl.pallas_call(
        _merge_kernel,
        grid=(t // tm,),
        in_specs=[row(D_MODEL), row(4 * D_MODEL), row(BRANCH), row(BRANCH), row(BRANCH), row(BRANCH),
                  pl.BlockSpec((4, D_MODEL), lambda i: (0, 0)),
                  pl.BlockSpec((4, BRANCH, D_MODEL), lambda i: (0, 0, 0)),
                  pl.BlockSpec((D_MODEL, D_MODEL), lambda i: (0, 0))],
        out_specs=row(D_MODEL),
        out_shape=jax.ShapeDtypeStruct((t, D_MODEL), F32),
        compiler_params=_cp(("parallel",)),
        name="merge_out",
    )(x, pg, o_a, o_b, o_c, o_d, gate_b, w_branch, w_out)


def _mlp_kernel(x_ref, g_ref, up_ref, down_ref, fg_ref, o_ref, h_ref, acc_ref, *, final_norm):
    j = pl.program_id(1)

    @pl.when(j == 0)
    def _():
        x = x_ref[...]
        ms = jnp.mean(x * x, axis=-1, keepdims=True)
        h_ref[...] = (x * lax.rsqrt(ms + NORM_EPS) * g_ref[...]).astype(BF16)
        acc_ref[...] = x

    u = jnp.dot(h_ref[...], up_ref[...], preferred_element_type=F32)
    u = jnp.square(jnp.maximum(u, 0.0))
    acc_ref[...] += jnp.dot(u.astype(BF16), down_ref[...], preferred_element_type=F32)

    @pl.when(j == pl.num_programs(1) - 1)
    def _():
        y = acc_ref[...]
        if final_norm:
            ms = jnp.mean(y * y, axis=-1, keepdims=True)
            y = y * lax.rsqrt(ms + NORM_EPS) * fg_ref[...]
        o_ref[...] = y


def mlp_block(x, g, up, down, final_g, final_norm):
    t = x.shape[0]
    tm = min(MLP_TM, t)
    return pl.pallas_call(
        functools.partial(_mlp_kernel, final_norm=final_norm),
        grid=(t // tm, D_FF // MLP_FF),
        in_specs=[pl.BlockSpec((tm, D_MODEL), lambda i, j: (i, 0)),
                  pl.BlockSpec((1, D_MODEL), lambda i, j: (0, 0)),
                  pl.BlockSpec((D_MODEL, MLP_FF), lambda i, j: (0, j)),
                  pl.BlockSpec((MLP_FF, D_MODEL), lambda i, j: (j, 0)),
                  pl.BlockSpec((1, D_MODEL), lambda i, j: (0, 0))],
        out_specs=pl.BlockSpec((tm, D_MODEL), lambda i, j: (i, 0)),
        out_shape=jax.ShapeDtypeStruct((t, D_MODEL), F32),
        scratch_shapes=[pltpu.VMEM((tm, D_MODEL), BF16), pltpu.VMEM((tm, D_MODEL), F32)],
        compiler_params=_cp(("parallel", "arbitrary")),
        name="mlp_block",
    )(x, g, up, down, final_g)


def _partner_cols(w, heads, dh, rot):
    lead = w.shape[:-1]
    w3 = w.reshape(*lead, heads, dh)
    half = rot // 2
    part = jnp.concatenate([-w3[..., half:rot], w3[..., :half], jnp.zeros_like(w3[..., rot:])], axis=-1)
    return w3, part


def _interleave(w, heads, dh, rot):
    w3, part = _partner_cols(w, heads, dh, rot)
    return jnp.concatenate([w3, part], axis=-1).reshape(*w.shape[:-1], heads * 2 * dh)


def _prep_weights(w_in, rwkv_mu):
    n_a = 512 + 128 + 128 + 256 + 64 + 4
    n_b = 1696
    off_b, off_c, off_d = n_a, n_a + n_b, n_a + n_b + 512
    off_g = off_d + 1536
    wa = w_in[..., :n_a]
    q, k, v = wa[..., 0:512], wa[..., 512:640], wa[..., 640:768]
    qi, ki, wi = wa[..., 768:1024], wa[..., 1024:1088], wa[..., 1088:1092]
    pad = lambda w, n: jnp.pad(w, [(0, 0)] * (w.ndim - 1) + [(0, n - w.shape[-1])])
    w_a = jnp.concatenate([_interleave(q, ATT_HEADS, ATT_DH, ROT_DIM), _interleave(qi, IDX_HEADS, IDX_DH, ROT_DIM),
                           _interleave(k, ATT_KV, ATT_DH, ROT_DIM), _interleave(ki, 1, IDX_DH, ROT_DIM),
                           v, pad(wi, 128)], axis=-1)
    w_a = pad(w_a, PA_W)
    wb = w_in[..., off_b:off_c]
    perm = lambda z: jnp.concatenate([z[..., 0:512], z[..., 544:1056], z[..., 1056:1568],
                                      z[..., 512:544], z[..., 1568:1600], z[..., 1600:1696]], axis=-1)
    w_b = pad(perm(wb), PB_W)
    mu = pad(perm(rwkv_mu), PB_W)
    w_c = w_in[..., off_c:off_d]
    wd = w_in[..., off_d:off_g]
    rq, rk, rv, rg = wd[..., 0:256], wd[..., 256:512], wd[..., 512:1024], wd[..., 1024:1536]
    w_d = jnp.concatenate([_interleave(rq, RET_H, RET_DK, RET_DK), _interleave(rk, RET_H, RET_DK, RET_DK), rv, rg],
                          axis=-1)
    w_g = w_in[..., off_g:]
    cast = lambda w: w.astype(BF16)
    return cast(w_a), cast(w_b), cast(w_c), cast(w_d), cast(w_g), mu


def kernel(x, positions, attn_norm_g, w_in, rwkv_mu, rwkv_w0, rwkv_w2, rwkv_a0, rwkv_a2, rwkv_g2,
           rwkv_k_k, rwkv_k_a, rwkv_r_k, rwkv_ln_g, rwkv_ln_b, pool_w, pool_scale, ret_gn_g,
           gate_b, w_branch, w_out, mlp_norm_g, mlp_up, mlp_down, final_norm_g):
    b, s, d = x.shape
    depth = w_in.shape[0]
    t = b * s
    w_a, w_b, w_c, w_d, w_g, mu = _prep_weights(w_in, rwkv_mu)
    zeros = jnp.zeros_like(rwkv_w0)
    vecs = jnp.stack([rwkv_w0, rwkv_a0, rwkv_k_k, rwkv_k_a, rwkv_r_k, rwkv_ln_g, rwkv_ln_b, zeros], axis=1)
    lora = jnp.zeros((depth, 3, 256, BRANCH), F32)
    lora = lora.at[:, 0, 0:RW_DL].set(rwkv_w2)
    lora = lora.at[:, 1, RW_DL:RW_DL + RW_AL].set(rwkv_a2)
    lora = lora.at[:, 2, RW_DL + RW_AL:RW_DL + RW_AL + RW_GL].set(rwkv_g2)
    lora = lora.astype(BF16)
    pool_w16, w_branch16, w_out16 = pool_w.astype(BF16), w_branch.astype(BF16), w_out.astype(BF16)
    up16, down16 = mlp_up.astype(BF16), mlp_down.astype(BF16)

    cos_a, sin_a, cos_r, sin_r = rope_tables(positions)
    xf = x.reshape(t, d)
    for l in range(depth):
        g = attn_norm_g[l][None]
        pa = norm_matmul(xf, g, w_a[l], F32)
        pb = norm_matmul(xf, g, w_b[l], BF16)
        pc = norm_matmul(xf, g, w_c[l], BF16)
        pd = norm_matmul(xf, g, w_d[l], BF16)
        pg = norm_matmul(xf, g, w_g[l], BF16)
        q, qi, k, ki, vt, wi = dsa_prep(pa, cos_a, sin_a, b, s)
        o_a = dsa_attention(q, qi, wi, k, ki, vt, b, s)
        o_b = rwkv_mix(pb, mu[l][None], vecs[l], lora[l], b, s)
        o_c = pool_mix(pc, pool_w16[l], pool_scale[l][None], b, s)
        o_d = ret_mix(pd, cos_r, sin_r, ret_gn_g[l][None], b, s)
        xf = merge_out(xf, pg, o_a, o_b, o_c, o_d, gate_b[l], w_branch16[l], w_out16[l])
        xf = mlp_block(xf, mlp_norm_g[l][None], up16[l], down16[l], final_norm_g[None], l == depth - 1)
    return xf.reshape(b, s, d)
```

```python
import functools
import math

import jax
import jax.numpy as jnp
from jax import lax
from jax.experimental import pallas as pl
from jax.experimental.pallas import tpu as pltpu

F32 = jnp.float32
BF16 = jnp.bfloat16
I32 = jnp.int32

D_MODEL = 1024
BRANCH = 512
ATT_HEADS, ATT_KV, ATT_DH, ROT_DIM = 8, 2, 64, 16
ROPE_THETA = 500000.0
IDX_HEADS, IDX_DH, INDEX_TOPK, Q_BLOCK = 4, 64, 256, 128
RW_N, RW_H = 64, 8
RW_DL, RW_AL, RW_GL = 32, 32, 96
RW_GN_EPS = 64e-5
POOL_WINDOWS = (2, 4, 8, 16)
POOL_GD = 128
RET_H, RET_DK, RET_DV = 8, 32, 64
RET_THETA = 10000.0
RET_GN_EPS = 1e-6
D_FF = 4096
NORM_EPS = 1e-5

LANES = 128
VMEM_LIMIT = 56 * 1024 * 1024

PROJ_TM = 1024
DSA_KC = 512
RW_C = 64
RW_TT = 512
RW_PAR = 4
POOL_TS = 512
RET_C = 256
MERGE_TM = 512
MLP_TM = 1024
MLP_FF = 1024

INT_MIN = -2 ** 31
SUBLANES = 8
GROUP_KEYS = 32 * SUBLANES
LOG2E = 1.4426950408889634
NEG_BIG = -1e30

PA_W = 2304
PA_Q, PA_QI, PA_K, PA_KI, PA_V, PA_WI = 0, 1024, 1536, 1792, 1920, 2048
PB_W = 1792
PD_W = 2048


def _cp(sem, vmem=VMEM_LIMIT):
    return pltpu.CompilerParams(dimension_semantics=sem, vmem_limit_bytes=vmem)


def _dot(a, b, dims=(((1,), (0,)), ((), ())), exact=False):
    if exact:
        return lax.dot_general(a.astype(F32), b.astype(F32), dims, precision=lax.Precision.HIGHEST,
                               preferred_element_type=F32)
    return lax.dot_general(a.astype(BF16), b.astype(BF16), dims, preferred_element_type=F32)


_NN = (((1,), (0,)), ((), ()))
_NT = (((1,), (1,)), ((), ()))
_TN = (((0,), (0,)), ((), ()))


def _dot_split(x, ones_rhs, dims=_NN, left=False):
    hi = x.astype(BF16)
    lo = (x - hi.astype(F32)).astype(BF16)
    if left:
        return (lax.dot_general(ones_rhs, hi, dims, preferred_element_type=F32)
                + lax.dot_general(ones_rhs, lo, dims, preferred_element_type=F32))
    return (lax.dot_general(hi, ones_rhs, dims, preferred_element_type=F32)
            + lax.dot_general(lo, ones_rhs, dims, preferred_element_type=F32))


def _norm_matmul_kernel(x_ref, g_ref, w_ref, o_ref, h_ref):
    @pl.when(pl.program_id(1) == 0)
    def _():
        x = x_ref[...]
        ms = jnp.mean(x * x, axis=-1, keepdims=True)
        h_ref[...] = (x * lax.rsqrt(ms + NORM_EPS) * g_ref[...]).astype(BF16)

    o_ref[...] = jnp.dot(h_ref[...], w_ref[...], preferred_element_type=F32).astype(o_ref.dtype)


def _pick_tn(n):
    for tn in (1024, 896, 768, 640, 512, 384, 256, 128):
        if n % tn == 0:
            return tn
    raise ValueError(n)


def norm_matmul(x, g, w, out_dtype):
    t, d = x.shape
    n = w.shape[1]
    tm, tn = min(PROJ_TM, t), _pick_tn(n)
    return pl.pallas_call(
        _norm_matmul_kernel,
        grid=(t // tm, n // tn),
        in_specs=[pl.BlockSpec((tm, d), lambda i, j: (i, 0)),
                  pl.BlockSpec((1, d), lambda i, j: (0, 0)),
                  pl.BlockSpec((d, tn), lambda i, j: (0, j))],
        out_specs=pl.BlockSpec((tm, tn), lambda i, j: (i, j)),
        out_shape=jax.ShapeDtypeStruct((t, n), out_dtype),
        scratch_shapes=[pltpu.VMEM((tm, d), BF16)],
        compiler_params=_cp(("parallel", "arbitrary")),
        name="norm_matmul",
    )(x, g, w)


def _rope_table_kernel(pos_ref, fa_ref, fr_ref, ma_ref, mr_ref, ca_ref, sa_ref, cr_ref, sr_ref):
    pos = pos_ref[...].astype(F32)
    ang_a = pos * fa_ref[...]
    ca_ref[...] = jnp.cos(ang_a) * ma_ref[...]
    sa_ref[...] = jnp.sin(ang_a) * ma_ref[...]
    ang_r = pos * fr_ref[...]
    cr_ref[...] = jnp.cos(ang_r) * mr_ref[...]
    sr_ref[...] = jnp.sin(ang_r) * mr_ref[...]


def rope_tables(positions):
    t = positions.size
    pos = positions.reshape(t, 1)
    inv_a = ROPE_THETA ** (-jnp.arange(0, ROT_DIM, 2, dtype=F32) / ROT_DIM)
    fa = jnp.concatenate([inv_a, inv_a, jnp.zeros((LANES - ROT_DIM,), F32)])[None]
    ma = jnp.concatenate([jnp.ones((ATT_DH,), F32), jnp.zeros((LANES - ATT_DH,), F32)])[None]
    inv_r = 1.0 / (RET_THETA ** jnp.linspace(0.0, 1.0, RET_DK // 2, dtype=F32))
    fr64 = jnp.concatenate([inv_r, inv_r, jnp.zeros((RET_DK,), F32)])
    fr = jnp.concatenate([fr64, fr64])[None]
    mr64 = jnp.concatenate([jnp.ones((RET_DK,), F32), jnp.zeros((RET_DK,), F32)])
    mr = jnp.concatenate([mr64, mr64])[None]
    ts = min(1024, t)
    row = pl.BlockSpec((1, LANES), lambda i: (0, 0))
    tab = pl.BlockSpec((ts, LANES), lambda i: (i, 0))
    return pl.pallas_call(
        _rope_table_kernel,
        grid=(t // ts,),
        in_specs=[pl.BlockSpec((ts, 1), lambda i: (i, 0)), row, row, row, row],
        out_specs=[tab, tab, tab, tab],
        out_shape=[jax.ShapeDtypeStruct((t, LANES), F32)] * 4,
        compiler_params=_cp(("parallel",)),
        name="rope_tables",
    )(pos, fa, fr, ma, mr)


def _rope_apply(p, cos, sin, half_block):
    w = p.shape[-1]
    reps = w // cos.shape[-1]
    c = jnp.concatenate([cos] * reps, axis=-1) if reps > 1 else cos
    s = jnp.concatenate([sin] * reps, axis=-1) if reps > 1 else sin
    partner = pltpu.roll(p, w - half_block, axis=1)
    return p * c + partner * s


def _dsa_prep_kernel(pa_ref, cos_ref, sin_ref, q_ref, qi_ref, k_ref, ki_ref, vt_ref, wi_ref):
    cos, sin = cos_ref[0], sin_ref[0]
    q_scale = ATT_DH ** -0.5 * LOG2E
    q_ref[0] = (_rope_apply(pa_ref[0, :, PA_Q:PA_Q + 1024], cos, sin, ATT_DH) * q_scale).astype(BF16)
    qi_ref[0] = _rope_apply(pa_ref[0, :, PA_QI:PA_QI + 512], cos, sin, ATT_DH).astype(BF16)
    k_ref[0] = _rope_apply(pa_ref[0, :, PA_K:PA_K + 256], cos, sin, ATT_DH).astype(BF16)
    ki_ref[0] = _rope_apply(pa_ref[0, :, PA_KI:PA_KI + 128], cos, sin, ATT_DH).astype(BF16)
    vt = pa_ref[0, :, PA_V:PA_V + 128].T
    ones = jnp.ones((ATT_DH, vt.shape[1]), F32)
    vt_ref[0] = jnp.concatenate([vt[:ATT_DH], ones, vt[ATT_DH:], ones], axis=0).astype(BF16)
    wi_ref[0] = pa_ref[0, :, PA_WI:PA_WI + 128]


def dsa_prep(pa, cos_a, sin_a, b, s):
    tq = min(512, s)
    blk = lambda w: pl.BlockSpec((1, tq, w), lambda bi, i: (bi, i, 0))
    r3 = lambda a: a.reshape(b, s, a.shape[-1])
    shp = lambda w, dt: jax.ShapeDtypeStruct((b, s, w), dt)
    return pl.pallas_call(
        _dsa_prep_kernel,
        grid=(b, s // tq),
        in_specs=[blk(PA_W), blk(LANES), blk(LANES)],
        out_specs=[blk(1024), blk(512), blk(256), blk(128),
                   pl.BlockSpec((1, 256, tq), lambda bi, i: (bi, 0, i)), blk(128)],
        out_shape=[shp(1024, BF16), shp(512, BF16), shp(256, BF16), shp(128, BF16),
                   jax.ShapeDtypeStruct((b, 256, s), BF16), shp(128, F32)],
        compiler_params=_cp(("parallel", "parallel")),
        name="dsa_prep",
    )(r3(pa), r3(cos_a), r3(sin_a))


def _bit_transpose32(words):
    a = list(words)
    for sh, mask in ((16, 0x0000FFFF), (8, 0x00FF00FF), (4, 0x0F0F0F0F), (2, 0x33333333), (1, 0x55555555)):
        for k in range(32):
            if k & sh:
                continue
            t = (jnp.right_shift(a[k], sh) ^ a[k + sh]) & mask
            a[k + sh] = a[k + sh] ^ t
            a[k] = a[k] ^ jnp.left_shift(t, sh)
    return a


def _dsa_kernel(q_ref, qi_ref, wi_ref, k_ref, ki_ref, vt_ref, o_ref, keys_ref, planes_ref, live_ref, *, kc, topk):
    qb = Q_BLOCK
    i = pl.program_id(1)
    n_kc = (i * qb + qb + kc - 1) // kc
    rep = ATT_HEADS // ATT_KV

    @pl.when(i == 0)
    def _():
        keys_ref[...] = jnp.full(keys_ref.shape, INT_MIN, I32)
        planes_ref[...] = jnp.zeros(planes_ref.shape, I32)

    qi = qi_ref[0]
    qis = jnp.concatenate([qi[:, h * LANES:(h + 1) * LANES] for h in range(IDX_HEADS)], axis=0)
    wit = wi_ref[0].T
    w_all = jnp.concatenate([wit[h:h + 1, :] for h in range(IDX_HEADS)], axis=1)
    kpos = lax.broadcasted_iota(I32, (kc, qb), 0)
    qpos = lax.broadcasted_iota(I32, (kc, qb), 1) + i * qb

    def score_body(c, carry):
        kic = ki_ref[0, pl.ds(pl.multiple_of(c * kc, kc), kc), :]
        d = lax.dot_general(kic, qis, _NT, preferred_element_type=F32)
        d = jnp.maximum(d, 0.0) * w_all
        acc = d[:, 0:qb]
        for h in range(1, IDX_HEADS):
            acc = acc + d[:, h * qb:(h + 1) * qb]
        acc = jnp.where(acc == 0.0, 0.0, acc)
        bits = pltpu.bitcast(acc, I32)
        key = jnp.where(bits < 0, bits ^ 0x7FFFFFFF, bits)
        key = jnp.where(kpos + c * kc <= qpos, key, INT_MIN)
        keys_ref[c] = key
        u = key ^ INT_MIN
        for g in range(kc // GROUP_KEYS):
            words = _bit_transpose32([u[g * GROUP_KEYS + SUBLANES * j:g * GROUP_KEYS + SUBLANES * (j + 1), :]
                                      for j in range(32)])
            for p in range(32):
                planes_ref[p, c * (kc // GROUP_KEYS) + g] = words[p]
        return carry

    lax.fori_loop(0, n_kc, score_body, 0)

    n_groups = planes_ref.shape[1]
    n_live = n_kc * (kc // GROUP_KEYS)
    for g in range(n_groups):
        live_ref[g] = jnp.full((SUBLANES, qb), jnp.where(g < n_live, -1, 0), I32)
    kf = jnp.full((1, qb), float(topk), F32)

    def popcount_rows(words):
        cnt = lax.population_count(words)
        tot = cnt[0]
        for g in range(1, n_groups):
            tot = tot + cnt[g]
        return jnp.sum(tot.astype(F32), axis=0, keepdims=True)

    def bit_body(it, carry):
        n_gt, prefix = carry
        p = 31 - it
        plane = planes_ref[p]
        live = live_ref[...]
        n_one = popcount_rows(live & plane)
        take = (n_gt + n_one) >= kf
        live_ref[...] = live & (plane ^ jnp.where(take, 0, -1)[None])
        return jnp.where(take, n_gt, n_gt + n_one), prefix | jnp.where(take, jnp.left_shift(jnp.int32(1), p), 0)

    n_gt, prefix = lax.fori_loop(0, 32, bit_body, (jnp.zeros((1, qb), F32), jnp.zeros((1, qb), I32)))
    ans = prefix ^ INT_MIN
    need = kf - n_gt
    cnt_ans = n_gt + popcount_rows(live_ref[...])

    tie = (ans > INT_MIN) & (cnt_ans > kf)
    any_tie = jnp.max(jnp.where(tie, 1.0, 0.0)) > 0.0

    @pl.when(any_tie)
    def _():
        ansb = jnp.broadcast_to(ans, (kc, qb))
        tieb = jnp.broadcast_to(jnp.where(tie, 1.0, 0.0), (kc, qb)) > 0.0
        lower = (lax.broadcasted_iota(I32, (kc, kc), 0) >= lax.broadcasted_iota(I32, (kc, kc), 1))
        lower = jnp.where(lower, 1.0, 0.0).astype(BF16)

        def tie_body(c, run):
            blk = keys_ref[c]
            eq = (blk == ansb) & tieb
            eqf = jnp.where(eq, 1.0, 0.0).astype(BF16)
            pref = jnp.dot(lower, eqf, preferred_element_type=F32) + run
            keys_ref[c] = jnp.where(eq & (pref > need), INT_MIN, blk)
            return pref[kc - 1:kc, :]

        lax.fori_loop(0, n_kc, tie_body, jnp.zeros((1, qb), F32))

    thr = jnp.broadcast_to(jnp.maximum(ans, INT_MIN + 1), (kc, qb))

    q = q_ref[0]
    qg = [jnp.concatenate([q[:, (g * rep + r) * LANES:(g * rep + r + 1) * LANES] for r in range(rep)], axis=0)
          for g in range(ATT_KV)]

    def attend(first, n_chunks, carry):
        carry = list(carry)

        def prep(u):
            bias = jnp.where(keys_ref[first + u] >= thr, 0.0, NEG_BIG)
            keys_c = pl.ds(pl.multiple_of((first + u) * kc, kc), kc)
            return jnp.concatenate([bias] * rep, axis=1), keys_c

        def scores(g, chunk):
            bias, keys_c = chunk
            kg = k_ref[0, keys_c, g * LANES:(g + 1) * LANES]
            return lax.dot_general(kg, qg[g], _NT, preferred_element_type=F32) + bias

        def softmax_pv(g, st, chunk):
            m_old = carry[2 * g]
            m_new = jnp.maximum(m_old, jnp.max(st, axis=0, keepdims=True))
            pt = jnp.exp2(st - m_new).astype(BF16)
            pv = jnp.dot(vt_ref[0, g * LANES:(g + 1) * LANES, chunk[1]], pt, preferred_element_type=F32)
            carry[2 * g] = m_new
            carry[2 * g + 1] = carry[2 * g + 1] * jnp.exp2(m_old - m_new) + pv

        chunks = [prep(u) for u in range(n_chunks)]
        steps = [(g, ch) for ch in chunks for g in range(ATT_KV)]
        pending = [scores(*steps[0]), scores(*steps[1])]
        for n, (g, ch) in enumerate(steps):
            st = pending.pop(0)
            if n + 2 < len(steps):
                pending.append(scores(*steps[n + 2]))
            softmax_pv(g, st, ch)
        return tuple(carry)

    init = []
    for g in range(ATT_KV):
        init += [jnp.full((1, rep * qb), NEG_BIG, F32), jnp.zeros((LANES, rep * qb), F32)]
    res = lax.fori_loop(0, n_kc // 2, lambda j, carry: attend(2 * j, 2, carry), tuple(init))
    res = lax.cond(n_kc % 2 == 1, lambda carry: attend(n_kc - 1, 1, carry), lambda carry: carry, res)

    outs = []
    for g in range(ATT_KV):
        acc = res[2 * g + 1]
        ot = acc[:ATT_DH] / acc[ATT_DH:ATT_DH + 1]
        outs += [ot[:, r * qb:(r + 1) * qb] for r in range(rep)]
    o_ref[0] = jnp.concatenate(outs, axis=0).T.astype(o_ref.dtype)


def dsa_attention(q, qi, wi, k, ki, vt, b, s):
    kc = min(DSA_KC, s)
    topk = min(INDEX_TOPK, s // 4)
    qblk = lambda w: pl.BlockSpec((1, Q_BLOCK, w), lambda bi, i: (bi, i, 0))
    full = lambda w: pl.BlockSpec((1, s, w), lambda bi, i: (bi, 0, 0))
    out = pl.pallas_call(
        functools.partial(_dsa_kernel, kc=kc, topk=topk),
        grid=(b, s // Q_BLOCK),
        in_specs=[qblk(1024), qblk(512), qblk(128), full(256), full(128),
                  pl.BlockSpec((1, 256, s), lambda bi, i: (bi, 0, 0))],
        out_specs=qblk(BRANCH),
        out_shape=jax.ShapeDtypeStruct((b, s, BRANCH), BF16),
        scratch_shapes=[pltpu.VMEM((s // kc, kc, Q_BLOCK), I32),
                        pltpu.VMEM((32, s // GROUP_KEYS, SUBLANES, Q_BLOCK), I32),
                        pltpu.VMEM((s // GROUP_KEYS, SUBLANES, Q_BLOCK), I32)],
        compiler_params=_cp(("parallel", "arbitrary")),
        name="dsa_attention",
    )(q, qi, wi, k, ki, vt)
    return out.reshape(b * s, BRANCH)


def _rwkv_kernel(pb_ref, mu_ref, vec_ref, lora_ref, bones_ref, o_ref,
                 carry_ref, state_ref, r_s, k_s, v_s, kk_s, b_s, ld_s, y_s, *, exact):
    tt = pb_ref.shape[1]
    c_len = RW_C
    t_idx = pl.program_id(1)

    @pl.when(t_idx == 0)
    def _():
        carry_ref[...] = jnp.zeros_like(carry_ref)
        state_ref[...] = jnp.zeros_like(state_ref)

    p = pb_ref[0].astype(F32)
    prev = pltpu.roll(p, 1, axis=0)
    first = lax.broadcasted_iota(I32, p.shape, 0) == 0
    prev = jnp.where(first, jnp.broadcast_to(carry_ref[7:8, :], p.shape), prev)
    carry_ref[...] = p[tt - 8:, :]
    p = p + (prev - p) * mu_ref[...]

    w0, a0, k_k, k_a = vec_ref[0:1, :], vec_ref[1:2, :], vec_ref[2:3, :], vec_ref[3:4, :]
    r_k, ln_g, ln_b = vec_ref[4:5, :], vec_ref[5:6, :], vec_ref[6:7, :]
    bones = bones_ref[...]

    r = p[:, 0:512]
    k = p[:, 512:1024]
    v = p[:, 1024:1536]
    lo = p[:, 1536:1792]
    wl = _dot(jnp.tanh(lo), lora_ref[0])
    al = _dot(lo, lora_ref[1])
    g = _dot(jax.nn.sigmoid(lo), lora_ref[2])
    z = -(w0 + wl)
    w = -(jnp.maximum(z, 0.0) + jnp.log(1.0 + jnp.exp(-jnp.abs(z)))) - 0.5
    a = jax.nn.sigmoid(a0 + al)
    kk = k * k_k
    ssq = _dot_split(kk * kk, bones)
    kk = kk / jnp.maximum(jnp.sqrt(ssq), 1e-12)
    k = k * (1.0 + (a - 1.0) * k_a)
    r_s[...] = r
    k_s[...] = k
    v_s[...] = v
    kk_s[...] = kk
    b_s[...] = kk * a
    ld_s[...] = -jnp.exp(w)

    ii = lax.broadcasted_iota(I32, (c_len, c_len), 0)
    jj = lax.broadcasted_iota(I32, (c_len, c_len), 1)
    strict, incl = ii > jj, ii >= jj
    eye = jnp.where(ii == jj, 1.0, 0.0)
    tri = jnp.where(incl, 1.0, 0.0).astype(BF16)
    blk_masks = {1 << sh: (ii >> sh) == (jj >> sh) for sh in (3, 4, 5, 6)}

    dot = functools.partial(_dot, exact=exact)
    off_masks = [blk_masks[sz] & jnp.logical_not(blk_masks[sz // 2]) for sz in (16, 32, 64)]

    def chunk_body(ci, carry):
        probs = []
        for u in range(RW_PAR):
            rows = pl.ds(pl.multiple_of((ci * RW_PAR + u) * c_len, c_len), c_len)
            ld = ld_s[rows, :]
            cum = _dot_split(ld, tri, left=True)
            cum_end = cum[c_len - 1:c_len, :]
            e_inv, e_end = jnp.exp(-cum), jnp.exp(cum_end - cum)
            kc_, bc_ = k_s[rows, :], b_s[rows, :]
            kkt = kk_s[rows, :] * jnp.exp(cum - ld)
            rt = r_s[rows, :] * jnp.exp(cum)
            kd, bd = kc_ * e_inv, bc_ * e_inv
            kh, bh = kc_ * e_end, bc_ * e_end
            vc = v_s[rows, :]
            e_last = jnp.exp(cum_end)
            for h in range(RW_H):
                sl = slice(h * RW_N, (h + 1) * RW_N)
                probs.append(dict(rows=rows, sl=sl, h=h, kkt=kkt[:, sl], rt=rt[:, sl], kd=kd[:, sl], bd=bd[:, sl],
                                  kh=kh[:, sl], bh=bh[:, sl], v=vc[:, sl], e_last=e_last[:, sl]))
        for p in probs:
            big = dot(jnp.concatenate([p["kkt"], p["rt"]], axis=0),
                      jnp.concatenate([p["bd"], p["kd"]], axis=0), _NT)
            p["m_b"] = jnp.where(strict, big[:c_len, :c_len], 0.0)
            p["m_k"] = jnp.where(strict, big[:c_len, c_len:], 0.0)
            p["p_b"] = jnp.where(incl, big[c_len:, :c_len], 0.0)
            p["p_k"] = jnp.where(incl, big[c_len:, c_len:], 0.0)
            p["n8"] = jnp.where(blk_masks[8], p["m_b"], 0.0)
        for p in probs:
            p["n2"] = dot(p["n8"], p["n8"])
            p["mkv"] = dot(p["m_k"], p["v"])
        for p in probs:
            p["n4"] = dot(p["n2"], p["n2"])
            p["t"] = dot(eye - p["n8"], eye + p["n2"])
        for p in probs:
            p["t"] = dot(p["t"], eye + p["n4"])
        for om in off_masks:
            for p in probs:
                p["a"] = dot(p["t"], jnp.where(om, p["m_b"], 0.0))
            for p in probs:
                p["t"] = p["t"] - dot(p["a"], p["t"])
        for p in probs:
            p["wu"] = dot(p["t"], jnp.concatenate([p["kkt"], p["mkv"]], axis=1))
            p["pkv"] = dot(p["p_k"], p["v"])
        for p in probs:
            pwu = dot(p["p_b"], p["wu"])
            p["y0"] = p["pkv"] - pwu[:, RW_N:]
            p["gm"] = p["rt"] - pwu[:, :RW_N]
            p["bw"] = dot(p["bh"], p["wu"][:, :RW_N], _TN)
            p["psi_t"] = dot(jnp.concatenate([p["v"], p["wu"][:, RW_N:]], axis=0),
                             jnp.concatenate([p["kh"], -p["bh"]], axis=0), _TN)
        for p in probs:
            st = state_ref[p["h"]]
            y_s[p["rows"], p["sl"]] = p["y0"] + _dot(p["gm"], st, _NT, exact=False)
            state_ref[p["h"]] = st * p["e_last"] - _dot(st, p["bw"], _NT, exact=False) + p["psi_t"]
        return carry

    lax.fori_loop(0, tt // (c_len * RW_PAR), chunk_body, 0)

    y = y_s[...]
    r, k, v = r_s[...], k_s[...], v_s[...]
    mean = _dot_split(y, bones) * (1.0 / RW_N)
    yc = y - mean
    var = _dot_split(yc * yc, bones) * (1.0 / RW_N)
    y = yc * lax.rsqrt(var + RW_GN_EPS) * ln_g + ln_b
    y = y + _dot_split(r * k * r_k, bones) * v
    o_ref[0] = (y * g).astype(o_ref.dtype)


def rwkv_mix(pb, mu, vecs, lora, b, s, exact=False):
    tt = min(RW_TT, s)
    bones = (jnp.arange(BRANCH)[:, None] // RW_N == jnp.arange(BRANCH)[None, :] // RW_N).astype(BF16)
    const = lambda shape: pl.BlockSpec(shape, lambda bi, ti: (0,) * len(shape))
    sq = lambda: pltpu.VMEM((tt, BRANCH), F32)
    out = pl.pallas_call(
        functools.partial(_rwkv_kernel, exact=exact),
        grid=(b, s // tt),
        in_specs=[pl.BlockSpec((1, tt, PB_W), lambda bi, ti: (bi, ti, 0)),
                  const((1, PB_W)), const((8, BRANCH)), const((3, 256, BRANCH)), const((BRANCH, BRANCH))],
        out_specs=pl.BlockSpec((1, tt, BRANCH), lambda bi, ti: (bi, ti, 0)),
        out_shape=jax.ShapeDtypeStruct((b, s, BRANCH), BF16),
        scratch_shapes=[pltpu.VMEM((8, PB_W), F32), pltpu.VMEM((RW_H, RW_N, RW_N), F32),
                        sq(), sq(), sq(), sq(), sq(), sq(), sq()],
        compiler_params=_cp(("parallel", "arbitrary")),
        name="rwkv_mix",
    )(pb.reshape(b, s, PB_W), mu, vecs, lora, bones)
    return out.reshape(b * s, BRANCH)


def _pool_kernel(pc_ref, w_ref, sc_ref, o_ref, carry_ref):
    ts = pc_ref.shape[1]
    t_idx = pl.program_id(1)

    @pl.when(t_idx == 0)
    def _():
        carry_ref[...] = jnp.zeros_like(carry_ref)

    x = pc_ref[0].astype(F32)
    ext = jnp.concatenate([carry_ref[...], x], axis=0)
    carry_ref[...] = x[ts - 16:, :]
    n = ts + 16
    s2 = ext + pltpu.roll(ext, 1, axis=0)
    s4 = s2[:, 128:] + pltpu.roll(s2[:, 128:], 2, axis=0)
    s8 = s4[:, 128:] + pltpu.roll(s4[:, 128:], 4, axis=0)
    s16 = s8[:, 128:] + pltpu.roll(s8[:, 128:], 8, axis=0)
    sums = (s2[16:, 0:128], s4[16:, 0:128], s8[16:, 0:128], s16[16:, 0:128])
    step = (lax.broadcasted_iota(I32, (ts, POOL_GD), 0) + t_idx * ts + 1).astype(F32)
    outs = []
    for gi, win in enumerate(POOL_WINDOWS):
        mean = sums[gi] / jnp.minimum(step, float(win))
        pooled = mean - x[:, gi * POOL_GD:(gi + 1) * POOL_GD]
        outs.append(jnp.dot(pooled.astype(BF16), w_ref[gi], preferred_element_type=F32))
    del n
    o_ref[0] = (jnp.concatenate(outs, axis=1) * sc_ref[...]).astype(o_ref.dtype)


def pool_mix(pc, pool_w, pool_scale, b, s):
    ts = min(POOL_TS, s)
    out = pl.pallas_call(
        _pool_kernel,
        grid=(b, s // ts),
        in_specs=[pl.BlockSpec((1, ts, BRANCH), lambda bi, ti: (bi, ti, 0)),
                  pl.BlockSpec((4, POOL_GD, POOL_GD), lambda bi, ti: (0, 0, 0)),
                  pl.BlockSpec((1, BRANCH), lambda bi, ti: (0, 0))],
        out_specs=pl.BlockSpec((1, ts, BRANCH), lambda bi, ti: (bi, ti, 0)),
        out_shape=jax.ShapeDtypeStruct((b, s, BRANCH), BF16),
        scratch_shapes=[pltpu.VMEM((16, BRANCH), F32)],
        compiler_params=_cp(("parallel", "arbitrary")),
        name="pool_mix",
    )(pc.reshape(b, s, BRANCH), pool_w, pool_scale)
    return out.reshape(b * s, BRANCH)


def _ret_kernel(pd_ref, cos_ref, sin_ref, gn_ref, bones_ref, o_ref, state_ref, o_s, intra_ref):
    c_len = pd_ref.shape[1]
    t_idx = pl.program_id(1)

    log_gamma = [math.log(1.0 - 2.0 ** (-5.0 - h)) for h in range(RET_H)]

    @pl.when(t_idx == 0)
    def _():
        state_ref[...] = jnp.zeros_like(state_ref)
        ii = lax.broadcasted_iota(I32, (c_len, c_len), 0)
        jj = lax.broadcasted_iota(I32, (c_len, c_len), 1)
        diff = (ii - jj).astype(F32)
        for h in range(RET_H):
            intra_ref[h] = jnp.where(diff >= 0, jnp.exp(jnp.maximum(diff, 0.0) * log_gamma[h]), 0.0)

    cos, sin = cos_ref[0], sin_ref[0]
    q = _rope_apply(pd_ref[0, :, 0:512].astype(F32), cos, sin, RET_DK)
    k = _rope_apply(pd_ref[0, :, 512:1024].astype(F32), cos, sin, RET_DK) * (RET_DK ** -0.5)
    v = pd_ref[0, :, 1024:1536]
    gate = pd_ref[0, :, 1536:2048].astype(F32)

    pos = lax.broadcasted_iota(I32, (c_len, 1), 0).astype(F32)
    heads = [slice(h * 64, (h + 1) * 64) for h in range(RET_H)]
    scs = [_dot(q[:, sl], k[:, sl], _NT) for sl in heads]
    inter = [_dot(q[:, sl], state_ref[h]) for h, sl in enumerate(heads)]
    kv = [_dot(k[:, sl] * jnp.exp((c_len - 1.0 - pos) * log_gamma[h]), v[:, sl], _TN) for h, sl in enumerate(heads)]
    for h, sl in enumerate(heads):
        o = _dot(scs[h] * intra_ref[h], v[:, sl]) + inter[h] * jnp.exp((pos + 1.0) * log_gamma[h])
        state_ref[h] = state_ref[h] * math.exp(c_len * log_gamma[h]) + kv[h]
        o_s[:, sl] = o

    o = o_s[...]
    bones = bones_ref[...]
    mean = _dot_split(o, bones) * (1.0 / RET_DV)
    oc = o - mean
    var = _dot_split(oc * oc, bones) * (1.0 / RET_DV)
    o = oc * lax.rsqrt(var + RET_GN_EPS) * gn_ref[...]
    o_ref[0] = (jax.nn.silu(gate) * o).astype(o_ref.dtype)


def ret_mix(pd, cos_r, sin_r, gn_g, b, s):
    c_len = min(RET_C, s)
    bones = (jnp.arange(BRANCH)[:, None] // RET_DV == jnp.arange(BRANCH)[None, :] // RET_DV).astype(BF16)
    out = pl.pallas_call(
        _ret_kernel,
        grid=(b, s // c_len),
        in_specs=[pl.BlockSpec((1, c_len, PD_W), lambda bi, ti: (bi, ti, 0)),
                  pl.BlockSpec((1, c_len, LANES), lambda bi, ti: (bi, ti, 0)),
                  pl.BlockSpec((1, c_len, LANES), lambda bi, ti: (bi, ti, 0)),
                  pl.BlockSpec((1, BRANCH), lambda bi, ti: (0, 0)),
                  pl.BlockSpec((BRANCH, BRANCH), lambda bi, ti: (0, 0))],
        out_specs=pl.BlockSpec((1, c_len, BRANCH), lambda bi, ti: (bi, ti, 0)),
        out_shape=jax.ShapeDtypeStruct((b, s, BRANCH), BF16),
        scratch_shapes=[pltpu.VMEM((RET_H, 64, RET_DV), F32), pltpu.VMEM((c_len, BRANCH), F32),
                        pltpu.VMEM((RET_H, c_len, c_len), F32)],
        compiler_params=_cp(("parallel", "arbitrary")),
        name="ret_mix",
    )(pd.reshape(b, s, PD_W), cos_r.reshape(b, s, LANES), sin_r.reshape(b, s, LANES), gn_g, bones)
    return out.reshape(b * s, BRANCH)


def _merge_kernel(x_ref, pg_ref, oa_ref, ob_ref, oc_ref, od_ref, gb_ref, wb_ref, wo_ref, o_ref):
    merged = None
    for i, br in enumerate((oa_ref, ob_ref, oc_ref, od_ref)):
        gate = jax.nn.sigmoid(pg_ref[:, i * D_MODEL:(i + 1) * D_MODEL].astype(F32) + gb_ref[i:i + 1, :])
        term = gate * jnp.dot(br[...], wb_ref[i], preferred_element_type=F32)
        merged = term if merged is None else merged + term
    o_ref[...] = x_ref[...] + jnp.dot(merged.astype(BF16), wo_ref[...], preferred_element_type=F32)


def merge_out(x, pg, o_a, o_b, o_c, o_d, gate_b, w_branch, w_out):
    t = x.shape[0]
    tm = min(MERGE_TM, t)
    row = lambda w: pl.BlockSpec((tm, w), lambda i: (i, 0))
    return pl.pallas_call(
        _merge_kernel,
        grid=(t // tm,),
        in_specs=[row(D_MODEL), row(4 * D_MODEL), row(BRANCH), row(BRANCH), row(BRANCH), row(BRANCH),
                  pl.BlockSpec((4, D_MODEL), lambda i: (0, 0)),
                  pl.BlockSpec((4, BRANCH, D_MODEL), lambda i: (0, 0, 0)),
                  pl.BlockSpec((D_MODEL, D_MODEL), lambda i: (0, 0))],
        out_specs=row(D_MODEL),
        out_shape=jax.ShapeDtypeStruct((t, D_MODEL), F32),
        compiler_params=_cp(("parallel",)),
        name="merge_out",
    )(x, pg, o_a, o_b, o_c, o_d, gate_b, w_branch, w_out)


def _mlp_kernel(x_ref, g_ref, up_ref, down_ref, fg_ref, o_ref, h_ref, acc_ref, *, final_norm):
    j = pl.program_id(1)

    @pl.when(j == 0)
    def _():
        x = x_ref[...]
        ms = jnp.mean(x * x, axis=-1, keepdims=True)
        h_ref[...] = (x * lax.rsqrt(ms + NORM_EPS) * g_ref[...]).astype(BF16)
        acc_ref[...] = x

    u = jnp.dot(h_ref[...], up_ref[...], preferred_element_type=F32)
    u = jnp.square(jnp.maximum(u, 0.0))
    acc_ref[...] += jnp.dot(u.astype(BF16), down_ref[...], preferred_element_type=F32)

    @pl.when(j == pl.num_programs(1) - 1)
    def _():
        y = acc_ref[...]
        if final_norm:
            ms = jnp.mean(y * y, axis=-1, keepdims=True)
            y = y * lax.rsqrt(ms + NORM_EPS) * fg_ref[...]
        o_ref[...] = y


def mlp_block(x, g, up, down, final_g, final_norm):
    t = x.shape[0]
    tm = min(MLP_TM, t)
    return pl.pallas_call(
        functools.partial(_mlp_kernel, final_norm=final_norm),
        grid=(t // tm, D_FF // MLP_FF),
        in_specs=[pl.BlockSpec((tm, D_MODEL), lambda i, j: (i, 0)),
                  pl.BlockSpec((1, D_MODEL), lambda i, j: (0, 0)),
                  pl.BlockSpec((D_MODEL, MLP_FF), lambda i, j: (0, j)),
                  pl.BlockSpec((MLP_FF, D_MODEL), lambda i, j: (j, 0)),
                  pl.BlockSpec((1, D_MODEL), lambda i, j: (0, 0))],
        out_specs=pl.BlockSpec((tm, D_MODEL), lambda i, j: (i, 0)),
        out_shape=jax.ShapeDtypeStruct((t, D_MODEL), F32),
        scratch_shapes=[pltpu.VMEM((tm, D_MODEL), BF16), pltpu.VMEM((tm, D_MODEL), F32)],
        compiler_params=_cp(("parallel", "arbitrary")),
        name="mlp_block",
    )(x, g, up, down, final_g)


def _partner_cols(w, heads, dh, rot):
    lead = w.shape[:-1]
    w3 = w.reshape(*lead, heads, dh)
    half = rot // 2
    part = jnp.concatenate([-w3[..., half:rot], w3[..., :half], jnp.zeros_like(w3[..., rot:])], axis=-1)
    return w3, part


def _interleave(w, heads, dh, rot):
    w3, part = _partner_cols(w, heads, dh, rot)
    return jnp.concatenate([w3, part], axis=-1).reshape(*w.shape[:-1], heads * 2 * dh)


def _prep_weights(w_in, rwkv_mu):
    n_a = 512 + 128 + 128 + 256 + 64 + 4
    n_b = 1696
    off_b, off_c, off_d = n_a, n_a + n_b, n_a + n_b + 512
    off_g = off_d + 1536
    wa = w_in[..., :n_a]
    q, k, v = wa[..., 0:512], wa[..., 512:640], wa[..., 640:768]
    qi, ki, wi = wa[..., 768:1024], wa[..., 1024:1088], wa[..., 1088:1092]
    pad = lambda w, n: jnp.pad(w, [(0, 0)] * (w.ndim - 1) + [(0, n - w.shape[-1])])
    w_a = jnp.concatenate([_interleave(q, ATT_HEADS, ATT_DH, ROT_DIM), _interleave(qi, IDX_HEADS, IDX_DH, ROT_DIM),
                           _interleave(k, ATT_KV, ATT_DH, ROT_DIM), _interleave(ki, 1, IDX_DH, ROT_DIM),
                           v, pad(wi, 128)], axis=-1)
    w_a = pad(w_a, PA_W)
    wb = w_in[..., off_b:off_c]
    perm = lambda z: jnp.concatenate([z[..., 0:512], z[..., 544:1056], z[..., 1056:1568],
                                      z[..., 512:544], z[..., 1568:1600], z[..., 1600:1696]], axis=-1)
    w_b = pad(perm(wb), PB_W)
    mu = pad(perm(rwkv_mu), PB_W)
    w_c = w_in[..., off_c:off_d]
    wd = w_in[..., off_d:off_g]
    rq, rk, rv, rg = wd[..., 0:256], wd[..., 256:512], wd[..., 512:1024], wd[..., 1024:1536]
    w_d = jnp.concatenate([_interleave(rq, RET_H, RET_DK, RET_DK), _interleave(rk, RET_H, RET_DK, RET_DK), rv, rg],
                          axis=-1)
    w_g = w_in[..., off_g:]
    cast = lambda w: w.astype(BF16)
    return cast(w_a), cast(w_b), cast(w_c), cast(w_d), cast(w_g), mu


def kernel(x, positions, attn_norm_g, w_in, rwkv_mu, rwkv_w0, rwkv_w2, rwkv_a0, rwkv_a2, rwkv_g2,
           rwkv_k_k, rwkv_k_a, rwkv_r_k, rwkv_ln_g, rwkv_ln_b, pool_w, pool_scale, ret_gn_g,
           gate_b, w_branch, w_out, mlp_norm_g, mlp_up, mlp_down, final_norm_g):
    b, s, d = x.shape
    depth = w_in.shape[0]
    t = b * s
    w_a, w_b, w_c, w_d, w_g, mu = _prep_weights(w_in, rwkv_mu)
    zeros = jnp.zeros_like(rwkv_w0)
    vecs = jnp.stack([rwkv_w0, rwkv_a0, rwkv_k_k, rwkv_k_a, rwkv_r_k, rwkv_ln_g, rwkv_ln_b, zeros], axis=1)
    lora = jnp.zeros((depth, 3, 256, BRANCH), F32)
    lora = lora.at[:, 0, 0:RW_DL].set(rwkv_w2)
    lora = lora.at[:, 1, RW_DL:RW_DL + RW_AL].set(rwkv_a2)
    lora = lora.at[:, 2, RW_DL + RW_AL:RW_DL + RW_AL + RW_GL].set(rwkv_g2)
    lora = lora.astype(BF16)
    pool_w16, w_branch16, w_out16 = pool_w.astype(BF16), w_branch.astype(BF16), w_out.astype(BF16)
    up16, down16 = mlp_up.astype(BF16), mlp_down.astype(BF16)

    cos_a, sin_a, cos_r, sin_r = rope_tables(positions)
    xf = x.reshape(t, d)
    for l in range(depth):
        g = attn_norm_g[l][None]
        pa = norm_matmul(xf, g, w_a[l], F32)
        pb = norm_matmul(xf, g, w_b[l], BF16)
        pc = norm_matmul(xf, g, w_c[l], BF16)
        pd = norm_matmul(xf, g, w_d[l], BF16)
        pg = norm_matmul(xf, g, w_g[l], BF16)
        q, qi, k, ki, vt, wi = dsa_prep(pa, cos_a, sin_a, b, s)
        o_a = dsa_attention(q, qi, wi, k, ki, vt, b, s)
        o_b = rwkv_mix(pb, mu[l][None], vecs[l], lora[l], b, s)
        o_c = pool_mix(pc, pool_w16[l], pool_scale[l][None], b, s)
        o_d = ret_mix(pd, cos_r, sin_r, ret_gn_g[l][None], b, s)
        xf = merge_out(xf, pg, o_a, o_b, o_c, o_d, gate_b[l], w_branch16[l], w_out16[l])
        xf = mlp_block(xf, mlp_norm_g[l][None], up16[l], down16[l], final_norm_g[None], l == depth - 1)
    return xf.reshape(b, s, d)
```
